```python
import math
import jax, jax.numpy as jnp
from jax import lax
import numpy as np


D_MODEL = 2048
BATCH = 4
SEQ = 2048
DEPTH = 4

D_MIX = D_MODEL
N_EVEN = (DEPTH + 1) // 2
N_ODD = DEPTH // 2
A_WIDTH = D_MIX // 2
A_HEAD_DIM = 128
A_HEADS = A_WIDTH // A_HEAD_DIM
A_CONV = 4
CHUNK = 64
B_WIDTH = D_MIX - A_WIDTH
S5_GROUP = 16
S5_GROUPS = B_WIDTH // S5_GROUP
S5_STATE = 64
C_WIDTH = D_MIX // 2
C_CONV = 3
D_WIDTH = D_MIX - C_WIDTH
DA_HEAD_DIM = 128
DA_HEADS = D_WIDTH // (2 * DA_HEAD_DIM)
Q_BLOCK = 128
D_FF = ((8 * D_MODEL // 3 + 255) // 256) * 256
AB_IN = 4 * A_WIDTH + 2 * A_HEADS + B_WIDTH
CD_IN = 3 * C_WIDTH + 3 * D_WIDTH
ALPHA = (2.0 * DEPTH) ** 0.25
BETA = (8.0 * DEPTH) ** -0.25
EPS = 1e-5

kernel_name = "hybrid_deltanet_s5_shortconv_diffattn_deepnorm_adaln"


def layer_norm(x, g, b):
    xf = x.astype(jnp.float32)
    mu = jnp.mean(xf, axis=-1, keepdims=True)
    var = jnp.mean(jnp.square(xf - mu), axis=-1, keepdims=True)
    return ((xf - mu) * lax.rsqrt(var + EPS) * g + b).astype(x.dtype)


def rms_norm(x, g):
    xf = x.astype(jnp.float32)
    return xf * lax.rsqrt(jnp.mean(jnp.square(xf), axis=-1, keepdims=True) + EPS) * g


def l2_normalize(t):
    return t * lax.rsqrt(jnp.sum(jnp.square(t), axis=-1, keepdims=True) + 1e-6)


def causal_dwconv(x, w):
    K = w.shape[0]
    L = x.shape[1]
    xp = jnp.pad(x, ((0, 0), (K - 1, 0), (0, 0)))
    return sum(w[k] * xp[:, k:k + L] for k in range(K))


def chunk_gated_delta(q, k, v, g, beta):
    Bsz, H, L, dk = q.shape
    dv = v.shape[-1]
    n = L // CHUNK
    q = q.reshape(Bsz, H, n, CHUNK, dk)
    k = k.reshape(Bsz, H, n, CHUNK, dk)
    v = v.reshape(Bsz, H, n, CHUNK, dv)
    g = jnp.cumsum(g.reshape(Bsz, H, n, CHUNK), axis=-1)
    beta = beta.reshape(Bsz, H, n, CHUNK)
    tril = jnp.tril(jnp.ones((CHUNK, CHUNK), dtype=bool))
    strict = jnp.tril(jnp.ones((CHUNK, CHUNK), dtype=bool), k=-1)
    gdiff = g[..., :, None] - g[..., None, :]
    decay = jnp.where(tril, jnp.exp(jnp.where(tril, gdiff, 0.0)), 0.0)
    k_beta = k * beta[..., None]
    v_beta = v * beta[..., None]
    lower = jnp.where(strict, jnp.einsum('bhnid,bhnjd->bhnij', k_beta, k) * decay, 0.0)
    eye = jnp.eye(CHUNK, dtype=jnp.float32)
    T = lax.linalg.triangular_solve(eye + lower, jnp.broadcast_to(eye, lower.shape),
                                    left_side=True, lower=True, unit_diagonal=True)
    u = jnp.einsum('bhnij,bhnjd->bhnid', T, v_beta)
    w = jnp.einsum('bhnij,bhnjd->bhnid', T, k_beta * jnp.exp(g)[..., None])
    intra = jnp.where(tril, jnp.einsum('bhnid,bhnjd->bhnij', q, k) * decay, 0.0)
    q_decay = q * jnp.exp(g)[..., None]
    k_tail = k * jnp.exp(g[..., -1:] - g)[..., None]
    g_last = jnp.exp(g[..., -1])

    def step(S, inp):
        u_n, w_n, qd_n, a_n, kt_n, gl_n = inp
        v_new = u_n - jnp.einsum('bhik,bhkv->bhiv', w_n, S)
        o_n = jnp.einsum('bhik,bhkv->bhiv', qd_n, S) + jnp.einsum('bhij,bhjv->bhiv', a_n, v_new)
        S = S * gl_n[..., None, None] + jnp.einsum('bhik,bhiv->bhkv', kt_n, v_new)
        return S, o_n

    xs = (jnp.moveaxis(u, 2, 0), jnp.moveaxis(w, 2, 0), jnp.moveaxis(q_decay, 2, 0),
          jnp.moveaxis(intra, 2, 0), jnp.moveaxis(k_tail, 2, 0), jnp.moveaxis(g_last, 2, 0))
    S0 = jnp.zeros((Bsz, H, dk, dv), jnp.float32)
    _, o = lax.scan(step, S0, xs)
    return jnp.moveaxis(o, 0, 2).reshape(Bsz, H, L, dv)


def gated_deltanet(q, k, v, z, b, a, conv_w, a_log, dt_bias, norm_g):
    Bsz, L, _ = q.shape
    out_dtype = z.dtype
    qkv = jax.nn.silu(causal_dwconv(jnp.concatenate([q, k, v], axis=-1), conv_w)).astype(jnp.float32)
    q, k, v = jnp.split(qkv, 3, axis=-1)

    def heads(t):
        return t.reshape(Bsz, L, A_HEADS, A_HEAD_DIM).transpose(0, 2, 1, 3)

    q = l2_normalize(heads(q)) * (A_HEAD_DIM ** -0.5)
    k = l2_normalize(heads(k))
    v = heads(v)
    beta = jax.nn.sigmoid(b.astype(jnp.float32)).transpose(0, 2, 1)
    g = (-jnp.exp(a_log.astype(jnp.float32))
         * jax.nn.softplus(a.astype(jnp.float32) + dt_bias.astype(jnp.float32))).transpose(0, 2, 1)
    o = chunk_gated_delta(q, k, v, g, beta).transpose(0, 2, 1, 3)
    zh = z.astype(jnp.float32).reshape(Bsz, L, A_HEADS, A_HEAD_DIM)
    o = rms_norm(o, norm_g) * jax.nn.silu(zh)
    return o.reshape(Bsz, L, A_WIDTH).astype(out_dtype)


def _ssm_combine(left, right):
    a_l, b_l = left
    a_r, b_r = right
    return a_r * a_l, a_r * b_l + b_r


def s5_ssm(u, a_re, a_im, b_re, b_im, c_re, c_im, d, log_dt, w_glu):
    Bsz, L, _ = u.shape
    uf = u.astype(jnp.float32).reshape(Bsz, L, S5_GROUPS, S5_GROUP)
    lam = lax.complex(a_re.astype(jnp.float32), a_im.astype(jnp.float32))
    delta = jnp.exp(log_dt.astype(jnp.float32))[:, None]
    a_bar = jnp.exp(lam * delta)
    b_mat = lax.complex(b_re.astype(jnp.float32), b_im.astype(jnp.float32))
    b_bar = ((a_bar - 1.0) / lam)[:, :, None] * b_mat
    c_mat = lax.complex(c_re.astype(jnp.float32), c_im.astype(jnp.float32))
    bu = jnp.einsum('gpc,blgc->blgp', b_bar, uf.astype(jnp.complex64))
    a_seq = jnp.broadcast_to(a_bar, bu.shape)
    _, states = lax.associative_scan(_ssm_combine, (a_seq, bu), axis=1)
    y = jnp.einsum('gcp,blgp->blgc', c_mat, states).real + d.astype(jnp.float32) * uf
    y = jax.nn.gelu(y).reshape(Bsz, L, B_WIDTH)
    y = y * jax.nn.sigmoid(y @ w_glu.astype(jnp.float32))
    return y.astype(u.dtype)


def diff_attention(q, k, v, lq1, lk1, lq2, lk2, norm_g, lambda_init):
    Bsz, L, _ = q.shape
    out_dtype = v.dtype
    q = q.astype(jnp.float32).reshape(Bsz, L, DA_HEADS, 2, DA_HEAD_DIM).transpose(0, 2, 3, 1, 4)
    k = k.astype(jnp.float32).reshape(Bsz, L, DA_HEADS, 2, DA_HEAD_DIM).transpose(0, 2, 3, 1, 4)
    v = v.astype(jnp.float32).reshape(Bsz, L, DA_HEADS, 2 * DA_HEAD_DIM).transpose(0, 2, 1, 3)
    lam = (jnp.exp(jnp.sum(lq1.astype(jnp.float32) * lk1.astype(jnp.float32)))
           - jnp.exp(jnp.sum(lq2.astype(jnp.float32) * lk2.astype(jnp.float32))) + lambda_init)
    scale = DA_HEAD_DIM ** -0.5
    slopes = 2.0 ** (-8.0 * jnp.arange(1, DA_HEADS + 1, dtype=jnp.float32) / DA_HEADS)
    nb = L // Q_BLOCK
    q_blocks = jnp.moveaxis(q.reshape(Bsz, DA_HEADS, 2, nb, Q_BLOCK, DA_HEAD_DIM), 3, 0)
    kpos = jnp.arange(L)

    def one_block(args):
        qb, bi = args
        qpos = bi * Q_BLOCK + jnp.arange(Q_BLOCK)
        dist = qpos[:, None] - kpos[None, :]
        bias = -slopes[:, None, None] * dist.astype(jnp.float32)
        s = jnp.einsum('bhiqd,bhikd->bhiqk', qb, k) * scale + bias[None, :, None]
        s = jnp.where(dist >= 0, s, -jnp.inf)
        p = jax.nn.softmax(s, axis=-1)
        attn = p[:, :, 0] - lam * p[:, :, 1]
        return jnp.einsum('bhqk,bhkv->bhqv', attn, v)

    o = lax.map(one_block, (q_blocks, jnp.arange(nb)))
    o = jnp.moveaxis(o, 0, 2).reshape(Bsz, DA_HEADS, L, 2 * DA_HEAD_DIM)
    o = rms_norm(o, norm_g) * (1.0 - lambda_init)
    return o.transpose(0, 2, 1, 3).reshape(Bsz, L, D_WIDTH).astype(out_dtype)


def mixer_ab(h, w_in, w_out, conv_w, a_log, dt_bias, norm_g,
             a_re, a_im, b_re, b_im, c_re, c_im, d, log_dt, w_glu):
    proj = h @ w_in
    offs = [A_WIDTH, 2 * A_WIDTH, 3 * A_WIDTH, 4 * A_WIDTH,
            4 * A_WIDTH + A_HEADS, 4 * A_WIDTH + 2 * A_HEADS]
    q, k, v, z, b, a, u = jnp.split(proj, offs, axis=-1)
    ya = gated_deltanet(q, k, v, z, b, a, conv_w, a_log, dt_bias, norm_g)
    yb = s5_ssm(u, a_re, a_im, b_re, b_im, c_re, c_im, d, log_dt, w_glu)
    return jnp.concatenate([ya, yb], axis=-1) @ w_out


def mixer_cd(h, w_in, w_out, conv_w, lq1, lk1, lq2, lk2, norm_g, lambda_init):
    proj = h @ w_in
    offs = [C_WIDTH, 2 * C_WIDTH, 3 * C_WIDTH, 3 * C_WIDTH + D_WIDTH, 3 * C_WIDTH + 2 * D_WIDTH]
    gb, gc, xc, q, k, v = jnp.split(proj, offs, axis=-1)
    yc = gb * causal_dwconv(gc * xc, conv_w)
    yd = diff_attention(q, k, v, lq1, lk1, lq2, lk2, norm_g, lambda_init)
    return jnp.concatenate([yc, yd], axis=-1) @ w_out


def swiglu(h, w_gate, w_up, w_down):
    return (jax.nn.silu(h @ w_gate) * (h @ w_up)) @ w_down


def setup_inputs(seed: int = 0) -> dict:
    key = jax.random.key(seed)
    ks = iter(jax.random.split(key, 48))
    nrm = lambda shape, s: jax.random.normal(next(ks), shape, jnp.float32) * s
    inp = {}
    inp["x"] = nrm((BATCH, SEQ, D_MODEL), 1.0)
    inp["c"] = nrm((BATCH, D_MODEL), 1.0)
    inp["ada_w"] = nrm((DEPTH, D_MODEL, 6 * D_MODEL), 0.1 * D_MODEL ** -0.5)
    inp["ada_b"] = nrm((DEPTH, 6 * D_MODEL), 0.01)
    inp["ln1_g"] = 1.0 + nrm((DEPTH, D_MODEL), 0.02)
    inp["ln1_b"] = nrm((DEPTH, D_MODEL), 0.02)
    inp["ln2_g"] = 1.0 + nrm((DEPTH, D_MODEL), 0.02)
    inp["ln2_b"] = nrm((DEPTH, D_MODEL), 0.02)
    inp["ffn_w_gate"] = nrm((DEPTH, D_MODEL, D_FF), D_MODEL ** -0.5)
    inp["ffn_w_up"] = nrm((DEPTH, D_MODEL, D_FF), D_MODEL ** -0.5)
    inp["ffn_w_down"] = nrm((DEPTH, D_FF, D_MODEL), BETA * D_FF ** -0.5)
    inp["ab_w_in"] = nrm((N_EVEN, D_MODEL, AB_IN), D_MODEL ** -0.5)
    inp["ab_w_out"] = nrm((N_EVEN, D_MIX, D_MODEL), BETA * D_MIX ** -0.5)
    inp["dn_conv_w"] = nrm((N_EVEN, A_CONV, 3 * A_WIDTH), A_CONV ** -0.5)
    inp["dn_a_log"] = jnp.log(jax.random.uniform(next(ks), (N_EVEN, A_HEADS), jnp.float32, 1.0, 16.0))
    dt = jnp.exp(jax.random.uniform(next(ks), (N_EVEN, A_HEADS), jnp.float32,
                                    math.log(1e-3), math.log(1e-1)))
    inp["dn_dt_bias"] = dt + jnp.log(-jnp.expm1(-dt))
    inp["dn_norm_g"] = 1.0 + nrm((N_EVEN, A_HEAD_DIM), 0.02)
    inp["s5_a_re"] = -0.5 + nrm((N_EVEN, S5_GROUPS, S5_STATE), 0.01)
    inp["s5_a_im"] = math.pi * jnp.arange(S5_STATE, dtype=jnp.float32) + nrm((N_EVEN, S5_GROUPS, S5_STATE), 0.01)
    inp["s5_b_re"] = nrm((N_EVEN, S5_GROUPS, S5_STATE, S5_GROUP), (2 * S5_GROUP) ** -0.5)
    inp["s5_b_im"] = nrm((N_EVEN, S5_GROUPS, S5_STATE, S5_GROUP), (2 * S5_GROUP) ** -0.5)
    inp["s5_c_re"] = nrm((N_EVEN, S5_GROUPS, S5_GROUP, S5_STATE), 0.5)
    inp["s5_c_im"] = nrm((N_EVEN, S5_GROUPS, S5_GROUP, S5_STATE), 0.5)
    inp["s5_d"] = nrm((N_EVEN, S5_GROUPS, S5_GROUP), 1.0)
    inp["s5_log_dt"] = jax.random.uniform(next(ks), (N_EVEN, S5_GROUPS), jnp.float32,
                                          math.log(1e-3), math.log(1e-1))
    inp["s5_w_glu"] = nrm((N_EVEN, B_WIDTH, B_WIDTH), B_WIDTH ** -0.5)
    inp["cd_w_in"] = nrm((N_ODD, D_MODEL, CD_IN), D_MODEL ** -0.5)
    inp["cd_w_out"] = nrm((N_ODD, D_MIX, D_MODEL), BETA * D_MIX ** -0.5)
    inp["sc_conv_w"] = nrm((N_ODD, C_CONV, C_WIDTH), C_CONV ** -0.5)
    inp["da_lq1"] = nrm((N_ODD, DA_HEAD_DIM), 0.1)
    inp["da_lk1"] = nrm((N_ODD, DA_HEAD_DIM), 0.1)
    inp["da_lq2"] = nrm((N_ODD, DA_HEAD_DIM), 0.1)
    inp["da_lk2"] = nrm((N_ODD, DA_HEAD_DIM), 0.1)
    inp["da_norm_g"] = 1.0 + nrm((N_ODD, 2 * DA_HEAD_DIM), 0.02)
    return inp


def reference(x, c, ada_w, ada_b, ln1_g, ln1_b, ln2_g, ln2_b, ffn_w_gate, ffn_w_up, ffn_w_down,
              ab_w_in, ab_w_out, dn_conv_w, dn_a_log, dn_dt_bias, dn_norm_g,
              s5_a_re, s5_a_im, s5_b_re, s5_b_im, s5_c_re, s5_c_im, s5_d, s5_log_dt, s5_w_glu,
              cd_w_in, cd_w_out, sc_conv_w, da_lq1, da_lk1, da_lq2, da_lk2, da_norm_g):
    c_act = jax.nn.silu(c)
    for i in range(DEPTH):
        mod = c_act @ ada_w[i] + ada_b[i]
        sh1, sc1, g1, sh2, sc2, g2 = [m[:, None, :] for m in jnp.split(mod, 6, axis=-1)]
        h = x * (1.0 + sc1) + sh1
        j = i // 2
        if i % 2 == 0:
            y = mixer_ab(h, ab_w_in[j], ab_w_out[j], dn_conv_w[j], dn_a_log[j], dn_dt_bias[j],
                         dn_norm_g[j], s5_a_re[j], s5_a_im[j], s5_b_re[j], s5_b_im[j],
                         s5_c_re[j], s5_c_im[j], s5_d[j], s5_log_dt[j], s5_w_glu[j])
        else:
            lambda_init = 0.8 - 0.6 * math.exp(-0.3 * i)
            y = mixer_cd(h, cd_w_in[j], cd_w_out[j], sc_conv_w[j], da_lq1[j], da_lk1[j],
                         da_lq2[j], da_lk2[j], da_norm_g[j], lambda_init)
        x = layer_norm(ALPHA * x + (1.0 + g1) * y, ln1_g[i], ln1_b[i])
        h = x * (1.0 + sc2) + sh2
        y = swiglu(h, ffn_w_gate[i], ffn_w_up[i], ffn_w_down[i])
        x = layer_norm(ALPHA * x + (1.0 + g2) * y, ln2_g[i], ln2_b[i])
    return x
```

```python
import functools
import math

import jax
import jax.numpy as jnp
from jax import lax
from jax.experimental import pallas as pl
from jax.experimental.pallas import tpu as pltpu

F32 = jnp.float32
BF16 = jnp.bfloat16

DEPTH = 4
A_HEAD_DIM = 128
A_CONV = 4
CHUNK = 64
S5_GROUP = 16
S5_STATE = 64
S5_STEP = 16
C_CONV = 3
DA_HEAD_DIM = 128
ALPHA = (2.0 * DEPTH) ** 0.25
EPS = 1e-5
LANES = 128
VMEM_LIMIT_BYTES = 56 * 1024 * 1024


def _params(*sem):
    return pltpu.CompilerParams(dimension_semantics=sem, vmem_limit_bytes=VMEM_LIMIT_BYTES)


def _silu(x):
    return x * jax.nn.sigmoid(x)


def _dot(a, b):
    return jnp.dot(a, b, preferred_element_type=F32)


def _dot_nt(a, b):
    return lax.dot_general(a, b, (((1,), (1,)), ((), ())), preferred_element_type=F32)


def _dot_tn(a, b):
    return lax.dot_general(a, b, (((0,), (0,)), ((), ())), preferred_element_type=F32)


def _split3(x):
    hi = x.astype(BF16)
    r1 = x - hi.astype(F32)
    mid = r1.astype(BF16)
    lo = (r1 - mid.astype(F32)).astype(BF16)
    return hi, mid, lo


def _dot_f32(a, b):
    a0, a1, a2 = _split3(a)
    b0, b1, b2 = _split3(b)
    return (_dot(a0, b0) + (_dot(a0, b1) + _dot(a1, b0))
            + (_dot(a1, b1) + _dot(a0, b2) + _dot(a2, b0)))


def _ada_kernel(c_ref, w_ref, b_ref, o_ref):
    c = c_ref[...]
    ca = _silu(c).astype(BF16)
    o_ref[0] = _dot(ca, w_ref[0].astype(BF16)) + b_ref[0]


def _ada_mod(c_pad, ada_w, ada_b):
    depth, d, n = ada_w.shape
    rows = c_pad.shape[0]
    tn = 1024
    return pl.pallas_call(
        _ada_kernel,
        grid=(depth, n // tn),
        in_specs=[pl.BlockSpec((rows, d), lambda i, j: (0, 0)),
                  pl.BlockSpec((1, d, tn), lambda i, j: (i, 0, j)),
                  pl.BlockSpec((1, 1, tn), lambda i, j: (i, 0, j))],
        out_specs=pl.BlockSpec((1, rows, tn), lambda i, j: (i, 0, j)),
        out_shape=jax.ShapeDtypeStruct((depth, rows, n), F32),
        compiler_params=_params("arbitrary", "arbitrary"),
        name="ada_mod",
    )(c_pad, ada_w, ada_b.reshape(depth, 1, n))


def _modulate_kernel(x_ref, sc_ref, sh_ref, o_ref):
    o_ref[...] = (x_ref[...] * (1.0 + sc_ref[0]) + sh_ref[0]).astype(o_ref.dtype)


def _modulate(x2, sc, sh, seq):
    m, d = x2.shape
    tl = min(seq, 512)
    nl = seq // tl
    return pl.pallas_call(
        _modulate_kernel,
        grid=(m // tl,),
        in_specs=[pl.BlockSpec((tl, d), lambda i: (i, 0)),
                  pl.BlockSpec((1, 1, d), lambda i: (i // nl, 0, 0)),
                  pl.BlockSpec((1, 1, d), lambda i: (i // nl, 0, 0))],
        out_specs=pl.BlockSpec((tl, d), lambda i: (i, 0)),
        out_shape=jax.ShapeDtypeStruct((m, d), BF16),
        compiler_params=_params("arbitrary"),
        name="modulate",
    )(x2, sc, sh)


def _mm_kernel(a_ref, w_ref, o_ref):
    o_ref[...] = _dot(a_ref[...], w_ref[...]).astype(o_ref.dtype)


def _mm(a, w, out_dtype, tm, tn, name):
    m, k = a.shape
    n = w.shape[1]
    tm = min(tm, m)
    tn = min(tn, n)
    return pl.pallas_call(
        _mm_kernel,
        grid=(n // tn, m // tm),
        in_specs=[pl.BlockSpec((tm, k), lambda j, i: (i, 0)),
                  pl.BlockSpec((k, tn), lambda j, i: (0, j))],
        out_specs=pl.BlockSpec((tm, tn), lambda j, i: (i, j)),
        out_shape=jax.ShapeDtypeStruct((m, n), out_dtype),
        compiler_params=_params("arbitrary", "arbitrary"),
        name=name,
    )(a, w)


def _gate_up_kernel(a_ref, wg_ref, wu_ref, o_ref):
    a = a_ref[...]
    g = _dot(a, wg_ref[...])
    u = _dot(a, wu_ref[...])
    o_ref[...] = (_silu(g) * u).astype(o_ref.dtype)


def _gate_up(a, wg, wu, tm, tn):
    m, k = a.shape
    n = wg.shape[1]
    tm = min(tm, m)
    return pl.pallas_call(
        _gate_up_kernel,
        grid=(n // tn, m // tm),
        in_specs=[pl.BlockSpec((tm, k), lambda j, i: (i, 0)),
                  pl.BlockSpec((k, tn), lambda j, i: (0, j)),
                  pl.BlockSpec((k, tn), lambda j, i: (0, j))],
        out_specs=pl.BlockSpec((tm, tn), lambda j, i: (i, j)),
        out_shape=jax.ShapeDtypeStruct((m, n), BF16),
        compiler_params=_params("arbitrary", "arbitrary"),
        name="ffn_gate_up",
    )(a, wg, wu)


def _glu_kernel(y_ref, w_ref, o_ref):
    y = y_ref[...]
    t = _dot(y, w_ref[...])
    o_ref[...] = (y.astype(F32) * jax.nn.sigmoid(t)).astype(o_ref.dtype)


def _glu(y, w, tm):
    m, k = y.shape
    tm = min(tm, m)
    return pl.pallas_call(
        _glu_kernel,
        grid=(m // tm,),
        in_specs=[pl.BlockSpec((tm, k), lambda i: (i, 0)),
                  pl.BlockSpec((k, k), lambda i: (0, 0))],
        out_specs=pl.BlockSpec((tm, k), lambda i: (i, 0)),
        out_shape=jax.ShapeDtypeStruct((m, k), BF16),
        compiler_params=_params("arbitrary"),
        name="s5_glu",
    )(y, w)


def _res_ln_kernel(*refs, n_in, nk, has_next):
    a_refs = refs[:n_in]
    w_refs = refs[n_in:2 * n_in]
    x_ref, g_ref, lng_ref, lnb_ref = refs[2 * n_in:2 * n_in + 4]
    pos = 2 * n_in + 4
    if has_next:
        sc_ref, sh_ref = refs[pos:pos + 2]
        pos += 2
    xo_ref = refs[pos]
    pos += 1
    if has_next:
        ho_ref = refs[pos]
        pos += 1
    acc_ref = refs[pos] if nk > 1 else None

    part = _dot(a_refs[0][...], w_refs[0][...])
    for a_ref, w_ref in zip(a_refs[1:], w_refs[1:]):
        part = part + _dot(a_ref[...], w_ref[...])

    def epilogue(y):
        r = ALPHA * x_ref[...] + (1.0 + g_ref[0]) * y
        mu = jnp.mean(r, axis=-1, keepdims=True)
        rc = r - mu
        var = jnp.mean(rc * rc, axis=-1, keepdims=True)
        xn = rc * lax.rsqrt(var + EPS) * lng_ref[...] + lnb_ref[...]
        xo_ref[...] = xn
        if has_next:
            ho_ref[...] = (xn * (1.0 + sc_ref[0]) + sh_ref[0]).astype(ho_ref.dtype)

    if nk == 1:
        epilogue(part)
    else:
        k = pl.program_id(1)

        @pl.when(k == 0)
        def _():
            acc_ref[...] = part

        @pl.when(k > 0)
        def _():
            acc_ref[...] += part

        @pl.when(k == nk - 1)
        def _():
            epilogue(acc_ref[...])


def _res_ln(a_list, w, x2, gate, ln_g, ln_b, nxt, seq, tm, tk, name):
    m, d = x2.shape
    n_in = len(a_list)
    ka = a_list[0].shape[1]
    tm = min(tm, m, seq)
    tk = min(tk, ka)
    nk = ka // tk
    assert n_in == 1 or nk == 1
    nl = seq // tm
    has_next = nxt is not None
    in_specs = [pl.BlockSpec((tm, tk), lambda i, k: (i, k)) for _ in a_list]
    in_specs += [pl.BlockSpec((tk, d), functools.partial(lambda i, k, s: (s * nk + k, 0), s=s))
                 for s in range(n_in)]
    bvec = pl.BlockSpec((1, 1, d), lambda i, k: (i // nl, 0, 0))
    pvec = pl.BlockSpec((1, d), lambda i, k: (0, 0))
    in_specs += [pl.BlockSpec((tm, d), lambda i, k: (i, 0)), bvec, pvec, pvec]
    args = list(a_list) + [w] * n_in + [x2, gate, ln_g.reshape(1, d), ln_b.reshape(1, d)]
    out_shape = [jax.ShapeDtypeStruct((m, d), F32)]
    out_specs = [pl.BlockSpec((tm, d), lambda i, k: (i, 0))]
    if has_next:
        in_specs += [bvec, bvec]
        args += list(nxt)
        out_shape.append(jax.ShapeDtypeStruct((m, d), BF16))
        out_specs.append(pl.BlockSpec((tm, d), lambda i, k: (i, 0)))
    scratch = [pltpu.VMEM((tm, d), F32)] if nk > 1 else []
    out = pl.pallas_call(
        functools.partial(_res_ln_kernel, n_in=n_in, nk=nk, has_next=has_next),
        grid=(m // tm, nk),
        in_specs=in_specs,
        out_specs=out_specs,
        out_shape=out_shape,
        scratch_shapes=scratch,
        compiler_params=_params("arbitrary", "arbitrary"),
        name=name,
    )(*args)
    return (out[0], out[1]) if has_next else (out[0], None)


def _shift_rows(x, s, row):
    return jnp.where(row >= s, pltpu.roll(x, s, axis=0), 0.0)


def _conv_silu(x_ref, cw_ref):
    x = x_ref[...].astype(F32)
    w = cw_ref[...]
    row = lax.broadcasted_iota(jnp.int32, x.shape, 0)
    acc = x * w[A_CONV - 1:A_CONV, :]
    for s in range(1, A_CONV):
        acc = acc + _shift_rows(x, s, row) * w[A_CONV - 1 - s:A_CONV - s, :]
    return _silu(acc)


def _l2n(t):
    return t * lax.rsqrt(jnp.sum(t * t, axis=-1, keepdims=True) + 1e-6)


def _delta_kernel(q_ref, k_ref, v_ref, z_ref, cwq_ref, cwk_ref, cwv_ref, a_ref, b_ref,
                  alog_ref, dtb_ref, ng_ref, o_ref,
                  qd_s, k_s, v_s, gc_s, bt_s, u_s, w_s, kt_s, in_s, gl_s, *, nc):
    c_len = CHUNK
    q = _l2n(_conv_silu(q_ref, cwq_ref)) * (A_HEAD_DIM ** -0.5)
    k_s[...] = _l2n(_conv_silu(k_ref, cwk_ref))
    v_s[...] = _conv_silu(v_ref, cwv_ref)
    qd_s[...] = q

    a = a_ref[0, 0]
    b = b_ref[0, 0]
    sp_in = a + dtb_ref[0]
    softplus = jnp.maximum(sp_in, 0.0) + jnp.log1p(jnp.exp(-jnp.abs(sp_in)))
    g = -jnp.exp(alog_ref[0]) * softplus
    ri = lax.broadcasted_iota(jnp.int32, (c_len, c_len), 0)
    ci = lax.broadcasted_iota(jnp.int32, (c_len, c_len), 1)
    g0, g1, g2 = _split3(g)
    triu = (ri <= ci).astype(BF16)
    gc_s[...] = _dot(g0, triu) + _dot(g1, triu) + _dot(g2, triu)
    bt_s[...] = jax.nn.sigmoid(b)

    tril = ri >= ci
    strict = ri > ci
    eye = ri == ci
    eye_f = eye.astype(F32)

    def to_col(row):
        return jnp.sum(jnp.where(eye, row, 0.0), axis=1, keepdims=True)

    def prep(c, carry):
        r0 = pl.multiple_of(c * c_len, c_len)
        kc = k_s[pl.ds(r0, c_len), :]
        vc = v_s[pl.ds(r0, c_len), :]
        qc = qd_s[pl.ds(r0, c_len), :]
        grow = gc_s[pl.ds(c, 1), :]
        brow = bt_s[pl.ds(c, 1), :]
        gcol = to_col(grow)
        bcol = to_col(brow)
        decay = jnp.where(tril, jnp.exp(jnp.where(tril, gcol - grow, 0.0)), 0.0)
        kb = kc.astype(BF16)
        kk = _dot_nt(kb, kb)
        nmat = jnp.where(strict, kk * bcol * decay, 0.0)
        x = eye_f
        for j in range(c_len - 1):
            x = x - nmat[:, j:j + 1] * x[j:j + 1, :]
        tb = x * brow
        tw = tb * jnp.exp(grow)
        u_s[pl.ds(r0, c_len), :] = _dot(tb.astype(BF16), vc.astype(BF16))
        w_s[pl.ds(r0, c_len), :] = _dot(tw.astype(BF16), kb).astype(BF16)
        qk = _dot_nt(qc.astype(BF16), kb)
        in_s[c] = jnp.where(tril, qk * decay, 0.0).astype(BF16)
        glast = grow[:, c_len - 1:c_len]
        kt_s[pl.ds(r0, c_len), :] = (kc * jnp.exp(glast - gcol)).astype(BF16)
        qd_s[pl.ds(r0, c_len), :] = qc * jnp.exp(gcol)
        gl_s[pl.ds(c, 1), :] = jnp.broadcast_to(jnp.exp(glast), (1, LANES))
        return carry

    lax.fori_loop(0, nc, prep, 0)

    ng = ng_ref[...]

    def scan(c, s):
        r0 = pl.multiple_of(c * c_len, c_len)
        sb = s.astype(BF16)
        v_new = u_s[pl.ds(r0, c_len), :] - _dot(w_s[pl.ds(r0, c_len), :], sb)
        vb = v_new.astype(BF16)
        o = _dot(qd_s[pl.ds(r0, c_len), :].astype(BF16), sb) + _dot(in_s[c], vb)
        s = s * gl_s[pl.ds(c, 1), :] + _dot_tn(kt_s[pl.ds(r0, c_len), :], vb)
        on = o * lax.rsqrt(jnp.mean(o * o, axis=-1, keepdims=True) + EPS) * ng
        zc = z_ref[pl.ds(r0, c_len), :].astype(F32)
        o_ref[pl.ds(r0, c_len), :] = (on * _silu(zc)).astype(o_ref.dtype)
        return s

    lax.fori_loop(0, nc, scan, jnp.zeros((A_HEAD_DIM, A_HEAD_DIM), F32))


def _deltanet(proj, ba, conv_w, a_log, dt_bias, norm_g, bsz, seq):
    dh = A_HEAD_DIM
    heads = a_log.shape[0]
    nc = seq // CHUNK
    bh = ba[:, :2 * heads].reshape(bsz, seq, 2, heads).transpose(2, 0, 3, 1).reshape(2, bsz, heads, nc, CHUNK)
    col = lambda off: pl.BlockSpec((seq, dh), lambda b, h, off=off: (b, off + h))
    cw = lambda off: pl.BlockSpec((A_CONV, dh), lambda b, h, off=off: (0, off + h))
    rows = pl.BlockSpec((1, 1, nc, CHUNK), lambda b, h: (b, h, 0, 0))
    scal = pl.BlockSpec((1, 1, 1), lambda b, h: (h, 0, 0))
    return pl.pallas_call(
        functools.partial(_delta_kernel, nc=nc),
        grid=(bsz, heads),
        in_specs=[col(0), col(heads), col(2 * heads), col(3 * heads),
                  cw(0), cw(heads), cw(2 * heads), rows, rows, scal, scal,
                  pl.BlockSpec((1, dh), lambda b, h: (0, 0))],
        out_specs=pl.BlockSpec((seq, dh), lambda b, h: (b, h)),
        out_shape=jax.ShapeDtypeStruct((bsz * seq, heads * dh), BF16),
        scratch_shapes=[pltpu.VMEM((seq, dh), F32), pltpu.VMEM((seq, dh), F32), pltpu.VMEM((seq, dh), F32),
                        pltpu.VMEM((nc, CHUNK), F32), pltpu.VMEM((nc, CHUNK), F32),
                        pltpu.VMEM((seq, dh), F32), pltpu.VMEM((seq, dh), BF16), pltpu.VMEM((seq, dh), BF16),
                        pltpu.VMEM((nc, CHUNK, CHUNK), BF16), pltpu.VMEM((nc, LANES), F32)],
        compiler_params=_params("arbitrary", "arbitrary"),
        name="gated_deltanet",
    )(proj, proj, proj, proj, conv_w, conv_w, conv_w, bh[1], bh[0],
      a_log.reshape(heads, 1, 1), dt_bias.reshape(heads, 1, 1), norm_g.reshape(1, dh))


def _cexp(re, im):
    m = jnp.exp(re)
    return m * jnp.cos(im), m * jnp.sin(im)


def _cmul(ar, ai, br, bi):
    return ar * br - ai * bi, ar * bi + ai * br


def _gelu_tanh(x):
    return 0.5 * x * (1.0 + jnp.tanh(math.sqrt(2.0 / math.pi) * (x + 0.044715 * (x * x * x))))


def _s5_kernel(u_ref, bre_ref, bim_ref, cre_ref, cim_ref, lr_r_ref, li_r_ref, dt_r_ref,
               lr_c_ref, li_c_ref, dt_c_ref, d_ref, o_ref, *, nc):
    step, grp, st = S5_STEP, S5_GROUP, S5_STATE
    wid, st2 = step * grp, 2 * S5_STATE
    gshift = grp.bit_length() - 1
    u = u_ref[0]
    r = u.shape[0]

    lr, li = lr_r_ref[0], li_r_ref[0]
    dt = jnp.exp(dt_r_ref[0])
    zr, zi = lr * dt, li * dt
    abr, abi = _cexp(zr, zi)
    den = lr * lr + li * li
    cr = ((abr - 1.0) * lr + abi * li) / den
    ci_ = (abi * lr - (abr - 1.0) * li) / den
    bbr, bbi = _cmul(cr, ci_, bre_ref[0], bim_ref[0])
    s_idx = lax.shift_right_logical(lax.broadcasted_iota(jnp.int32, (wid, st2), 0), gshift).astype(F32)
    pr, pi_ = _cexp(-s_idx * zr, -s_idx * zi)
    ptr, pti = _cmul(pr, pi_, bbr, bbi)
    a_last_r, a_last_i = _cexp((step - 1.0) * zr, (step - 1.0) * zi)
    bpr, bpi = _cmul(ptr, pti, a_last_r, a_last_i)
    re_lane = lax.broadcasted_iota(jnp.int32, (wid, st2), 1) < st
    p2 = jnp.where(re_lane, ptr, pti)
    bp2 = jnp.where(re_lane, bpr, bpi)

    lrc, lic = lr_c_ref[0], li_c_ref[0]
    dtc = jnp.exp(dt_c_ref[0])
    zrc, zic = lrc * dtc, lic * dtc
    t_idx = lax.shift_right_logical(lax.broadcasted_iota(jnp.int32, (st2, wid), 1), gshift).astype(F32)
    er, ei = _cexp(t_idx * zrc, t_idx * zic)
    qr, qi = _cmul(er, ei, cre_ref[0], cim_ref[0])
    a1r, a1i = _cexp(zrc, zic)
    cqr, cqi = _cmul(qr, qi, a1r, a1i)
    re_row = lax.broadcasted_iota(jnp.int32, (st2, wid), 0) < st
    q2 = jnp.where(re_row, qr, -qi)
    cq2 = jnp.where(re_row, cqr, -cqi)

    srow = lax.shift_right_logical(lax.broadcasted_iota(jnp.int32, (wid, wid), 0), gshift)
    tcol = lax.shift_right_logical(lax.broadcasted_iota(jnp.int32, (wid, wid), 1), gshift)
    mmat = jnp.where(srow <= tcol, _dot_f32(p2, q2), 0.0)

    x = _dot(u, bp2.astype(BF16))
    blk = lax.broadcasted_iota(jnp.int32, (r, st2), 0) & (nc - 1)
    re_vec = lax.broadcasted_iota(jnp.int32, (1, st2), 1) < st
    sh = 1
    while sh < nc:
        ekr, eki = _cexp((sh * step) * zr, (sh * step) * zi)
        xs = jnp.where(blk >= sh, pltpu.roll(x, sh, axis=0), 0.0)
        x = x + xs * ekr + pltpu.roll(xs, st, axis=1) * jnp.where(re_vec, -eki, eki)
        sh *= 2
    xp = jnp.where(blk >= 1, pltpu.roll(x, 1, axis=0), 0.0)

    y = _dot(u, mmat.astype(BF16)) + _dot(xp.astype(BF16), cq2.astype(BF16)) + d_ref[0] * u.astype(F32)
    o_ref[0] = _gelu_tanh(y).astype(o_ref.dtype)


def _s5(u2, a_re, a_im, b_re, b_im, c_re, c_im, d, log_dt, bsz, seq):
    groups, st = a_re.shape
    grp, step = S5_GROUP, S5_STEP
    wid, st2 = step * grp, 2 * st
    nc = seq // step
    assert nc & (nc - 1) == 0
    r = bsz * nc
    ug = u2.reshape(r, step, groups, grp).transpose(2, 0, 1, 3).reshape(groups, r, wid)
    tile_b = lambda t: jnp.tile(t.transpose(0, 2, 1), (1, step, 2))
    tile_c = lambda t: jnp.tile(t.transpose(0, 2, 1), (1, 2, step))
    row = lambda t: jnp.tile(t.reshape(groups, 1, st), (1, 1, 2))
    colv = lambda t: jnp.tile(t.reshape(groups, st, 1), (1, 2, 1))
    ldt = jnp.broadcast_to(log_dt[:, None], (groups, st))
    g3 = lambda shp: pl.BlockSpec((1,) + shp, lambda g: (g, 0, 0))
    out = pl.pallas_call(
        functools.partial(_s5_kernel, nc=nc),
        grid=(groups,),
        in_specs=[g3((r, wid)), g3((wid, st2)), g3((wid, st2)), g3((st2, wid)), g3((st2, wid)),
                  g3((1, st2)), g3((1, st2)), g3((1, st2)), g3((st2, 1)), g3((st2, 1)), g3((st2, 1)),
                  g3((1, wid))],
        out_specs=g3((r, wid)),
        out_shape=jax.ShapeDtypeStruct((groups, r, wid), BF16),
        compiler_params=_params("arbitrary"),
        name="s5_ssm",
    )(ug, tile_b(b_re), tile_b(b_im), tile_c(c_re), tile_c(c_im),
      row(a_re), row(a_im), row(ldt), colv(a_re), colv(a_im), colv(ldt),
      jnp.tile(d, (1, step)).reshape(groups, 1, wid))
    return out.reshape(groups, r, step, grp).transpose(1, 2, 0, 3).reshape(bsz * seq, groups * grp)


def _sconv_kernel(gb_ref, gc_ref, xc_ref, w_ref, o_ref):
    p = gc_ref[...].astype(F32) * xc_ref[...].astype(F32)
    w = w_ref[...]
    row = lax.broadcasted_iota(jnp.int32, p.shape, 0)
    acc = p * w[C_CONV - 1:C_CONV, :]
    for s in range(1, C_CONV):
        acc = acc + _shift_rows(p, s, row) * w[C_CONV - 1 - s:C_CONV - s, :]
    o_ref[...] = (gb_ref[...].astype(F32) * acc).astype(o_ref.dtype)


def _sconv(proj, conv_w, bsz, seq):
    width = conv_w.shape[1]
    tc = 256
    nb = width // tc
    col = lambda off: pl.BlockSpec((seq, tc), lambda b, j, off=off: (b, off * nb + j))
    return pl.pallas_call(
        _sconv_kernel,
        grid=(bsz, nb),
        in_specs=[col(0), col(1), col(2), pl.BlockSpec((C_CONV, tc), lambda b, j: (0, j))],
        out_specs=pl.BlockSpec((seq, tc), lambda b, j: (b, j)),
        out_shape=jax.ShapeDtypeStruct((bsz * seq, width), BF16),
        compiler_params=_params("arbitrary", "arbitrary"),
        name="short_conv",
    )(proj, proj, proj, conv_w)


NEG_BIG = -1e30


def _dattn_kernel(q_ref, k_ref, v_ref, lq1_ref, lk1_ref, lq2_ref, lk2_ref, ng_ref, o_ref,
                  acc_s, *, heads, tq, lambda_init):
    dh = DA_HEAD_DIM
    qi = pl.program_id(1)
    scale = dh ** -0.5
    lam = (jnp.exp(jnp.sum(lq1_ref[...] * lk1_ref[...], axis=-1, keepdims=True))
           - jnp.exp(jnp.sum(lq2_ref[...] * lk2_ref[...], axis=-1, keepdims=True)) + lambda_init)
    ri = lax.broadcasted_iota(jnp.int32, (tq, tq), 0)
    ci = lax.broadcasted_iota(jnp.int32, (tq, tq), 1)
    rel = (ri - ci).astype(F32)
    causal = ri >= ci
    ng = ng_ref[...]

    for h in range(heads):
        slope = 2.0 ** (-8.0 * (h + 1) / heads)
        bias0 = -slope * rel
        v_lo = h * 2 * dh
        outs = []
        for mp in range(2):
            q_lo = v_lo + mp * dh
            qm = q_ref[:, q_lo:q_lo + dh]

            def block(j, carry, masked, qm=qm, q_lo=q_lo):
                m, l = carry
                k0 = pl.multiple_of(j * tq, tq)
                km = k_ref[pl.ds(k0, tq), q_lo:q_lo + dh]
                vm = v_ref[pl.ds(k0, tq), v_lo:v_lo + 2 * dh]
                s = _dot_nt(qm, km) * scale + (bias0 - slope * ((qi - j) * tq).astype(F32))
                if masked:
                    s = jnp.where(causal, s, NEG_BIG)
                m_new = jnp.maximum(m, jnp.max(s, axis=-1, keepdims=True))
                corr = jnp.exp(m - m_new)
                p = jnp.exp(s - m_new)
                l = corr * l + jnp.sum(p, axis=-1, keepdims=True)
                acc_s[mp] = corr * acc_s[mp] + _dot(p.astype(BF16), vm)
                return m_new, l

            acc_s[mp] = jnp.zeros((tq, 2 * dh), F32)
            carry = (jnp.full((tq, 1), NEG_BIG, F32), jnp.zeros((tq, 1), F32))
            carry = lax.fori_loop(0, qi, functools.partial(block, masked=False), carry)
            m, l = block(qi, carry, True)
            outs.append(acc_s[mp] / l)
        o = outs[0] - lam * outs[1]
        on = o * lax.rsqrt(jnp.mean(o * o, axis=-1, keepdims=True) + EPS) * ng * (1.0 - lambda_init)
        o_ref[:, v_lo:v_lo + 2 * dh] = on.astype(o_ref.dtype)


def _dattn(proj, lq1, lk1, lq2, lk2, norm_g, lambda_init, bsz, seq):
    dh = DA_HEAD_DIM
    width = proj.shape[1] // 6
    heads = width // (2 * dh)
    tq = min(256, seq)
    nq = seq // tq
    vec = pl.BlockSpec((1, dh), lambda b, i: (0, 0))
    return pl.pallas_call(
        functools.partial(_dattn_kernel, heads=heads, tq=tq, lambda_init=lambda_init),
        grid=(bsz, nq),
        in_specs=[pl.BlockSpec((tq, width), lambda b, i: (b * nq + i, 3)),
                  pl.BlockSpec((seq, width), lambda b, i: (b, 4)),
                  pl.BlockSpec((seq, width), lambda b, i: (b, 5)),
                  vec, vec, vec, vec, pl.BlockSpec((1, 2 * dh), lambda b, i: (0, 0))],
        out_specs=pl.BlockSpec((tq, width), lambda b, i: (b * nq + i, 0)),
        out_shape=jax.ShapeDtypeStruct((bsz * seq, width), BF16),
        scratch_shapes=[pltpu.VMEM((2, tq, 2 * dh), F32)],
        compiler_params=_params("arbitrary", "arbitrary"),
        name="diff_attention",
    )(proj, proj, proj, lq1.reshape(1, dh), lk1.reshape(1, dh), lq2.reshape(1, dh), lk2.reshape(1, dh),
      norm_g.reshape(1, 2 * dh))


def kernel(x, c, ada_w, ada_b, ln1_g, ln1_b, ln2_g, ln2_b, ffn_w_gate, ffn_w_up, ffn_w_down, ab_w_in, ab_w_out, dn_conv_w, dn_a_log, dn_dt_bias, dn_norm_g, s5_a_re, s5_a_im, s5_b_re, s5_b_im, s5_c_re, s5_c_im, s5_d, s5_log_dt, s5_w_glu, cd_w_in, cd_w_out, sc_conv_w, da_lq1, da_lk1, da_lq2, da_lk2, da_norm_g):
    bsz, seq, d = x.shape
    depth = ada_w.shape[0]
    m = bsz * seq
    a_width = dn_conv_w.shape[-1] // 3
    heads = dn_a_log.shape[-1]

    c_pad = jnp.zeros((8, d), F32).at[:bsz].set(c)
    mod = _ada_mod(c_pad, ada_w, ada_b)

    def mvec(i, j):
        return mod[i, :bsz, j * d:(j + 1) * d].reshape(bsz, 1, d)

    x2 = x.reshape(m, d)
    h = _modulate(x2, mvec(0, 1), mvec(0, 0), seq)
    for i in range(depth):
        j = i // 2
        if i % 2 == 0:
            w_in = ab_w_in[j]
            w_main = jnp.concatenate([w_in[:, :4 * a_width], w_in[:, 4 * a_width + 2 * heads:]], axis=1).astype(BF16)
            w_ba = jnp.pad(w_in[:, 4 * a_width:4 * a_width + 2 * heads], ((0, 0), (0, LANES - 2 * heads))).astype(BF16)
            proj = _mm(h, w_main, BF16, 1024, 1024, "ab_in_proj")
            ba = _mm(h, w_ba, F32, 1024, LANES, "ab_gate_proj")
            ya = _deltanet(proj, ba, dn_conv_w[j], dn_a_log[j], dn_dt_bias[j], dn_norm_g[j], bsz, seq)
            yb = _s5(proj[:, 4 * a_width:], s5_a_re[j], s5_a_im[j], s5_b_re[j], s5_b_im[j],
                     s5_c_re[j], s5_c_im[j], s5_d[j], s5_log_dt[j], bsz, seq)
            yb = _glu(yb, s5_w_glu[j].astype(BF16), 1024)
            w_out = ab_w_out[j].astype(BF16)
        else:
            lambda_init = 0.8 - 0.6 * math.exp(-0.3 * i)
            proj = _mm(h, cd_w_in[j].astype(BF16), BF16, 1024, 1024, "cd_in_proj")
            ya = _sconv(proj, sc_conv_w[j], bsz, seq)
            yb = _dattn(proj, da_lq1[j], da_lk1[j], da_lq2[j], da_lk2[j], da_norm_g[j], lambda_init, bsz, seq)
            w_out = cd_w_out[j].astype(BF16)
        x2, h = _res_ln([ya, yb], w_out, x2, mvec(i, 2), ln1_g[i], ln1_b[i],
                        (mvec(i, 4), mvec(i, 3)), seq, 512, ya.shape[1], "mixer_out_ln")
        act = _gate_up(h, ffn_w_gate[i].astype(BF16), ffn_w_up[i].astype(BF16), 1024, 512)
        nxt = (mvec(i + 1, 1), mvec(i + 1, 0)) if i + 1 < depth else None
        x2, h = _res_ln([act], ffn_w_down[i].astype(BF16), x2, mvec(i, 5), ln2_g[i], ln2_b[i],
                        nxt, seq, 512, 1408, "ffn_down_ln")
    return x2.reshape(bsz, seq, d)
```

```python
import functools
import math

import jax
import jax.numpy as jnp
from jax import lax
from jax.experimental import pallas as pl
from jax.experimental.pallas import tpu as pltpu

F32 = jnp.float32
BF16 = jnp.bfloat16

DEPTH = 4
A_HEAD_DIM = 128
A_CONV = 4
CHUNK = 64
S5_GROUP = 16
S5_STATE = 64
S5_STEP = 16
C_CONV = 3
DA_HEAD_DIM = 128
ALPHA = (2.0 * DEPTH) ** 0.25
EPS = 1e-5
LANES = 128
VMEM_LIMIT_BYTES = 56 * 1024 * 1024


def _params(*sem):
    return pltpu.CompilerParams(dimension_semantics=sem, vmem_limit_bytes=VMEM_LIMIT_BYTES)


def _silu(x):
    return x * jax.nn.sigmoid(x)


def _dot(a, b):
    return jnp.dot(a, b, preferred_element_type=F32)


def _dot_nt(a, b):
    return lax.dot_general(a, b, (((1,), (1,)), ((), ())), preferred_element_type=F32)


def _dot_tn(a, b):
    return lax.dot_general(a, b, (((0,), (0,)), ((), ())), preferred_element_type=F32)


def _split3(x):
    hi = x.astype(BF16)
    r1 = x - hi.astype(F32)
    mid = r1.astype(BF16)
    lo = (r1 - mid.astype(F32)).astype(BF16)
    return hi, mid, lo


def _dot_f32(a, b):
    a0, a1, a2 = _split3(a)
    b0, b1, b2 = _split3(b)
    return (_dot(a0, b0) + (_dot(a0, b1) + _dot(a1, b0))
            + (_dot(a1, b1) + _dot(a0, b2) + _dot(a2, b0)))


def _ada_kernel(c_ref, w_ref, b_ref, o_ref):
    c = c_ref[...]
    ca = _silu(c).astype(BF16)
    o_ref[0] = _dot(ca, w_ref[0].astype(BF16)) + b_ref[0]


def _ada_mod(c_pad, ada_w, ada_b):
    depth, d, n = ada_w.shape
    rows = c_pad.shape[0]
    tn = 1024
    return pl.pallas_call(
        _ada_kernel,
        grid=(depth, n // tn),
        in_specs=[pl.BlockSpec((rows, d), lambda i, j: (0, 0)),
                  pl.BlockSpec((1, d, tn), lambda i, j: (i, 0, j)),
                  pl.BlockSpec((1, 1, tn), lambda i, j: (i, 0, j))],
        out_specs=pl.BlockSpec((1, rows, tn), lambda i, j: (i, 0, j)),
        out_shape=jax.ShapeDtypeStruct((depth, rows, n), F32),
        compiler_params=_params("arbitrary", "arbitrary"),
        name="ada_mod",
    )(c_pad, ada_w, ada_b.reshape(depth, 1, n))


def _modulate_kernel(x_ref, sc_ref, sh_ref, o_ref):
    o_ref[...] = (x_ref[...] * (1.0 + sc_ref[0]) + sh_ref[0]).astype(o_ref.dtype)


def _modulate(x2, sc, sh, seq):
    m, d = x2.shape
    tl = min(seq, 512)
    nl = seq // tl
    return pl.pallas_call(
        _modulate_kernel,
        grid=(m // tl,),
        in_specs=[pl.BlockSpec((tl, d), lambda i: (i, 0)),
                  pl.BlockSpec((1, 1, d), lambda i: (i // nl, 0, 0)),
                  pl.BlockSpec((1, 1, d), lambda i: (i // nl, 0, 0))],
        out_specs=pl.BlockSpec((tl, d), lambda i: (i, 0)),
        out_shape=jax.ShapeDtypeStruct((m, d), BF16),
        compiler_params=_params("arbitrary"),
        name="modulate",
    )(x2, sc, sh)


def _mm_kernel(a_ref, w_ref, o_ref, w_s):
    @pl.when(pl.program_id(1) == 0)
    def _():
        w_s[...] = w_ref[0].astype(BF16)

    o_ref[...] = _dot(a_ref[...], w_s[...]).astype(o_ref.dtype)


def _mm(a, w, layer, n, out_dtype, tm, tn, name):
    m, k = a.shape
    tm = min(tm, m)
    tn = min(tn, n)
    return pl.pallas_call(
        _mm_kernel,
        grid=(n // tn, m // tm),
        in_specs=[pl.BlockSpec((tm, k), lambda j, i: (i, 0)),
                  pl.BlockSpec((1, k, tn), lambda j, i: (layer, 0, j))],
        out_specs=pl.BlockSpec((tm, tn), lambda j, i: (i, j)),
        out_shape=jax.ShapeDtypeStruct((m, n), out_dtype),
        scratch_shapes=[pltpu.VMEM((k, tn), BF16)],
        compiler_params=_params("arbitrary", "arbitrary"),
        name=name,
    )(a, w)


def _gate_up_kernel(a_ref, wg_ref, wu_ref, o_ref, wg_s, wu_s):
    @pl.when(pl.program_id(1) == 0)
    def _():
        wg_s[...] = wg_ref[0].astype(BF16)
        wu_s[...] = wu_ref[0].astype(BF16)

    a = a_ref[...]
    g = _dot(a, wg_s[...])
    u = _dot(a, wu_s[...])
    o_ref[...] = (_silu(g) * u).astype(o_ref.dtype)


def _gate_up(a, wg, wu, layer, tm, tn):
    m, k = a.shape
    n = wg.shape[2]
    tm = min(tm, m)
    return pl.pallas_call(
        _gate_up_kernel,
        grid=(n // tn, m // tm),
        in_specs=[pl.BlockSpec((tm, k), lambda j, i: (i, 0)),
                  pl.BlockSpec((1, k, tn), lambda j, i: (layer, 0, j)),
                  pl.BlockSpec((1, k, tn), lambda j, i: (layer, 0, j))],
        out_specs=pl.BlockSpec((tm, tn), lambda j, i: (i, j)),
        out_shape=jax.ShapeDtypeStruct((m, n), BF16),
        scratch_shapes=[pltpu.VMEM((k, tn), BF16), pltpu.VMEM((k, tn), BF16)],
        compiler_params=_params("arbitrary", "arbitrary"),
        name="ffn_gate_up",
    )(a, wg, wu)


def _glu_kernel(y_ref, w_ref, o_ref):
    y = y_ref[...]
    t = _dot(y, w_ref[...])
    o_ref[...] = (y.astype(F32) * jax.nn.sigmoid(t)).astype(o_ref.dtype)


def _glu(y, w, tm):
    m, k = y.shape
    tm = min(tm, m)
    return pl.pallas_call(
        _glu_kernel,
        grid=(m // tm,),
        in_specs=[pl.BlockSpec((tm, k), lambda i: (i, 0)),
                  pl.BlockSpec((k, k), lambda i: (0, 0))],
        out_specs=pl.BlockSpec((tm, k), lambda i: (i, 0)),
        out_shape=jax.ShapeDtypeStruct((m, k), BF16),
        compiler_params=_params("arbitrary"),
        name="s5_glu",
    )(y, w)


def _res_ln_kernel(*refs, n_in, nk, has_next):
    a_refs = refs[:n_in]
    w_refs = refs[n_in:2 * n_in]
    x_ref, g_ref, lng_ref, lnb_ref = refs[2 * n_in:2 * n_in + 4]
    pos = 2 * n_in + 4
    if has_next:
        sc_ref, sh_ref = refs[pos:pos + 2]
        pos += 2
    xo_ref = refs[pos]
    pos += 1
    if has_next:
        ho_ref = refs[pos]
        pos += 1
    acc_ref = refs[pos] if nk > 1 else None

    part = _dot(a_refs[0][...], w_refs[0][...])
    for a_ref, w_ref in zip(a_refs[1:], w_refs[1:]):
        part = part + _dot(a_ref[...], w_ref[...])

    def epilogue(y):
        r = ALPHA * x_ref[...] + (1.0 + g_ref[0]) * y
        mu = jnp.mean(r, axis=-1, keepdims=True)
        rc = r - mu
        var = jnp.mean(rc * rc, axis=-1, keepdims=True)
        xn = rc * lax.rsqrt(var + EPS) * lng_ref[...] + lnb_ref[...]
        xo_ref[...] = xn
        if has_next:
            ho_ref[...] = (xn * (1.0 + sc_ref[0]) + sh_ref[0]).astype(ho_ref.dtype)

    if nk == 1:
        epilogue(part)
    else:
        k = pl.program_id(1)

        @pl.when(k == 0)
        def _():
            acc_ref[...] = part

        @pl.when(k > 0)
        def _():
            acc_ref[...] += part

        @pl.when(k == nk - 1)
        def _():
            epilogue(acc_ref[...])


def _res_ln(a_list, w, x2, gate, ln_g, ln_b, nxt, seq, tm, tk, name):
    m, d = x2.shape
    n_in = len(a_list)
    ka = a_list[0].shape[1]
    tm = min(tm, m, seq)
    tk = min(tk, ka)
    nk = ka // tk
    assert n_in == 1 or nk == 1
    nl = seq // tm
    has_next = nxt is not None
    in_specs = [pl.BlockSpec((tm, tk), lambda i, k: (i, k)) for _ in a_list]
    in_specs += [pl.BlockSpec((tk, d), functools.partial(lambda i, k, s: (s * nk + k, 0), s=s))
                 for s in range(n_in)]
    bvec = pl.BlockSpec((1, 1, d), lambda i, k: (i // nl, 0, 0))
    pvec = pl.BlockSpec((1, d), lambda i, k: (0, 0))
    in_specs += [pl.BlockSpec((tm, d), lambda i, k: (i, 0)), bvec, pvec, pvec]
    args = list(a_list) + [w] * n_in + [x2, gate, ln_g.reshape(1, d), ln_b.reshape(1, d)]
    out_shape = [jax.ShapeDtypeStruct((m, d), F32)]
    out_specs = [pl.BlockSpec((tm, d), lambda i, k: (i, 0))]
    if has_next:
        in_specs += [bvec, bvec]
        args += list(nxt)
        out_shape.append(jax.ShapeDtypeStruct((m, d), BF16))
        out_specs.append(pl.BlockSpec((tm, d), lambda i, k: (i, 0)))
    scratch = [pltpu.VMEM((tm, d), F32)] if nk > 1 else []
    out = pl.pallas_call(
        functools.partial(_res_ln_kernel, n_in=n_in, nk=nk, has_next=has_next),
        grid=(m // tm, nk),
        in_specs=in_specs,
        out_specs=out_specs,
        out_shape=out_shape,
        scratch_shapes=scratch,
        compiler_params=_params("arbitrary", "arbitrary"),
        name=name,
    )(*args)
    return (out[0], out[1]) if has_next else (out[0], None)


def _shift_rows(x, s, row):
    return jnp.where(row >= s, pltpu.roll(x, s, axis=0), 0.0)


def _conv_silu(x_ref, cw_ref):
    x = x_ref[...].astype(F32)
    w = cw_ref[...]
    row = lax.broadcasted_iota(jnp.int32, x.shape, 0)
    acc = x * w[A_CONV - 1:A_CONV, :]
    for s in range(1, A_CONV):
        acc = acc + _shift_rows(x, s, row) * w[A_CONV - 1 - s:A_CONV - s, :]
    return _silu(acc)


def _l2n(t):
    return t * lax.rsqrt(jnp.sum(t * t, axis=-1, keepdims=True) + 1e-6)


DN_HB = 4


def _delta_kernel(q_ref, k_ref, v_ref, z_ref, cwq_ref, cwk_ref, cwv_ref, a_ref, b_ref,
                  alog_ref, dtb_ref, ng_ref, o_ref,
                  q_s, k_s, v_s, gc_s, bt_s, u_s, wq_s, l2_s, gl_s, s_s, n_s, nt_s, x_s, *, nc, seq):
    c_len, dh = CHUNK, A_HEAD_DIM
    nmat_all = DN_HB * nc
    hh = lax.rem(pl.program_id(1), DN_HB)
    q_s[hh] = _l2n(_conv_silu(q_ref, cwq_ref)) * (dh ** -0.5)
    k_s[hh] = _l2n(_conv_silu(k_ref, cwk_ref))
    v_s[hh] = _conv_silu(v_ref, cwv_ref)

    a = a_ref[0, 0]
    b = b_ref[0, 0]
    sp_in = a + dtb_ref[0]
    softplus = jnp.maximum(sp_in, 0.0) + jnp.log1p(jnp.exp(-jnp.abs(sp_in)))
    g = -jnp.exp(alog_ref[0]) * softplus
    ri = lax.broadcasted_iota(jnp.int32, (c_len, LANES), 0)
    ci = lax.broadcasted_iota(jnp.int32, (c_len, LANES), 1)
    tril, strict, eye = ri >= ci, ri > ci, ri == ci
    eye64 = eye[:, :c_len]
    eye_bf = jnp.where(eye64, 1.0, 0.0).astype(BF16)
    g0, g1, g2 = _split3(g)
    triu = jnp.where(ci < c_len, jnp.where(ri <= ci, 1.0, 0.0), 0.0).astype(BF16)
    gc_s[hh] = _dot(g0, triu) + _dot(g1, triu) + _dot(g2, triu)
    bt_s[hh] = jax.nn.sigmoid(b)

    def prep(c, carry):
        r0 = pl.multiple_of(c * c_len, c_len)
        kc = k_s[hh, pl.ds(r0, c_len), :]
        qc = q_s[hh, pl.ds(r0, c_len), :]
        grow = gc_s[hh, pl.ds(c, 1), :]
        brow = bt_s[hh, pl.ds(c, 1), :]
        gcol = jnp.sum(jnp.where(eye, grow, 0.0), axis=1, keepdims=True)
        bcol = jnp.sum(jnp.where(eye64, brow, 0.0), axis=1, keepdims=True)
        decay = jnp.where(tril, jnp.exp(jnp.where(tril, gcol - grow, 0.0)), 0.0)
        kb = kc.astype(BF16)
        kk = _dot_nt(kb, jnp.concatenate([kb, jnp.zeros_like(kb)], axis=0))
        m0 = pl.multiple_of((hh * nc + c) * c_len, c_len)
        n_s[pl.ds(m0, c_len), :] = jnp.where(strict, kk * bcol * decay, 0.0)
        qk = _dot_nt(qc.astype(BF16), kb)
        glast = grow[:, c_len - 1:c_len]
        kt = (kc * jnp.exp(glast - gcol)).astype(BF16)
        wq_s[hh * nc + c, c_len:, :] = (qc * jnp.exp(gcol)).astype(BF16)
        l2_s[hh * nc + c, :c_len, :] = (qk * decay[:, :c_len]).astype(BF16)
        l2_s[hh * nc + c, c_len:, :] = _dot_tn(kt, eye_bf).astype(BF16)
        gl_s[hh, pl.ds(c, 1), :] = jnp.broadcast_to(jnp.exp(glast), (1, LANES))
        return carry

    lax.fori_loop(0, nc, prep, 0, unroll=2)

    @pl.when(hh == DN_HB - 1)
    def _():
        x_s[...] = jnp.zeros(x_s.shape, F32)
        for i in range(c_len):
            slab = n_s[pl.ds(i, nmat_all, stride=c_len), :]
            nt_s[i] = slab.T[:c_len, :]
        sub = lax.broadcasted_iota(jnp.int32, (8, LANES), 0)
        for ib in range(c_len // 8):
            npc = ib + 1

            def row(ii, carry, ib=ib, npc=npc):
                i = ib * 8 + ii
                parts = [[None] * 4 for _ in range(npc)]
                for j in range(8 * npc):
                    coef = nt_s[i, pl.ds(j, 1), :]
                    for p in range(j // 8 + 1):
                        term = coef * x_s[j, p * 8:(p + 1) * 8, :]
                        k = j % 4
                        parts[p][k] = term if parts[p][k] is None else parts[p][k] + term
                for p in range(npc):
                    live = [t for t in parts[p] if t is not None]
                    tot = live[0]
                    for t in live[1:]:
                        tot = tot + t
                    x_s[i, p * 8:(p + 1) * 8, :] = jnp.where(sub + p * 8 == i, 1.0, 0.0) - tot
                return carry

            lax.fori_loop(0, 8, row, 0)
        zpad = jnp.zeros((LANES - c_len, LANES), F32)
        for i in range(c_len):
            n_s[pl.ds(i, nmat_all, stride=c_len), :] = jnp.concatenate([x_s[i], zpad], axis=0).T

        nshift = nc.bit_length() - 1

        def uw(idx, carry):
            hd = lax.shift_right_logical(idx, nshift)
            c = idx & (nc - 1)
            r0 = pl.multiple_of(c * c_len, c_len)
            t = n_s[pl.ds(pl.multiple_of(idx * c_len, c_len), c_len), :][:, :c_len]
            tb = t * bt_s[hd, pl.ds(c, 1), :]
            tw = tb * jnp.exp(gc_s[hd, pl.ds(c, 1), :][:, :c_len])
            kb = k_s[hd, pl.ds(r0, c_len), :].astype(BF16)
            vb = v_s[hd, pl.ds(r0, c_len), :].astype(BF16)
            u_s[hd, pl.ds(r0, c_len), :] = _dot(tb.astype(BF16), vb)
            wq_s[idx, :c_len, :] = _dot(tw.astype(BF16), kb).astype(BF16)
            return carry

        lax.fori_loop(0, nmat_all, uw, 0, unroll=2)

        ng = ng_ref[...]
        s_s[...] = jnp.zeros(s_s.shape, F32)

        def scan(c, carry):
            r0 = pl.multiple_of(c * c_len, c_len)
            for hd in range(DN_HB):
                s = s_s[hd]
                sb = s.astype(BF16)
                r1 = _dot(wq_s[hd * nc + c], sb)
                v_new = u_s[hd, pl.ds(r0, c_len), :] - r1[:c_len]
                r2 = _dot(l2_s[hd * nc + c], v_new.astype(BF16))
                o = r1[c_len:] + r2[:c_len]
                s_s[hd] = s * gl_s[hd, pl.ds(c, 1), :] + r2[c_len:]
                u_s[hd, pl.ds(r0, c_len), :] = o * lax.rsqrt(jnp.mean(o * o, axis=-1, keepdims=True) + EPS) * ng
            return carry

        lax.fori_loop(0, nc, scan, 0)

        rt = 256

        def gate(r, carry):
            r0 = pl.multiple_of(r * rt, rt)
            for hd in range(DN_HB):
                zc = z_ref[pl.ds(r0, rt), hd * dh:(hd + 1) * dh].astype(F32)
                o_ref[pl.ds(r0, rt), hd * dh:(hd + 1) * dh] = (u_s[hd, pl.ds(r0, rt), :] * _silu(zc)).astype(o_ref.dtype)
            return carry

        lax.fori_loop(0, seq // rt, gate, 0)


def _deltanet(proj, ba, conv_w, a_log, dt_bias, norm_g, bsz, seq):
    dh = A_HEAD_DIM
    heads = a_log.shape[0]
    nc = seq // CHUNK
    assert DN_HB * nc == LANES and heads % DN_HB == 0
    bh = ba[:, :2 * heads].reshape(bsz, seq, 2, heads).transpose(2, 0, 3, 1).reshape(2, bsz, heads, nc, CHUNK)
    col = lambda off: pl.BlockSpec((seq, dh), lambda b, h, off=off: (b, off + h))
    cw = lambda off: pl.BlockSpec((A_CONV, dh), lambda b, h, off=off: (0, off + h))
    rows = pl.BlockSpec((1, 1, nc, CHUNK), lambda b, h: (b, h, 0, 0))
    scal = pl.BlockSpec((1, 1, 1), lambda b, h: (h, 0, 0))
    hgroups = heads // DN_HB
    return pl.pallas_call(
        functools.partial(_delta_kernel, nc=nc, seq=seq),
        grid=(bsz, heads),
        in_specs=[col(0), col(heads), col(2 * heads),
                  pl.BlockSpec((seq, DN_HB * dh), lambda b, h: (b, 3 * hgroups + h // DN_HB)),
                  cw(0), cw(heads), cw(2 * heads), rows, rows, scal, scal,
                  pl.BlockSpec((1, dh), lambda b, h: (0, 0))],
        out_specs=pl.BlockSpec((seq, DN_HB * dh), lambda b, h: (b, h // DN_HB)),
        out_shape=jax.ShapeDtypeStruct((bsz * seq, heads * dh), BF16),
        scratch_shapes=[pltpu.VMEM((DN_HB, seq, dh), F32), pltpu.VMEM((DN_HB, seq, dh), F32),
                        pltpu.VMEM((DN_HB, seq, dh), F32),
                        pltpu.VMEM((DN_HB, nc, LANES), F32), pltpu.VMEM((DN_HB, nc, CHUNK), F32),
                        pltpu.VMEM((DN_HB, seq, dh), F32),
                        pltpu.VMEM((DN_HB * nc, 2 * CHUNK, dh), BF16), pltpu.VMEM((DN_HB * nc, CHUNK + dh, CHUNK), BF16),
                        pltpu.VMEM((DN_HB, nc, LANES), F32), pltpu.VMEM((DN_HB, dh, dh), F32),
                        pltpu.VMEM((DN_HB * nc * CHUNK, LANES), F32), pltpu.VMEM((CHUNK, CHUNK, LANES), F32),
                        pltpu.VMEM((CHUNK, CHUNK, LANES), F32)],
        compiler_params=_params("arbitrary", "arbitrary"),
        name="gated_deltanet",
    )(proj, proj, proj, proj, conv_w, conv_w, conv_w, bh[1], bh[0],
      a_log.reshape(heads, 1, 1), dt_bias.reshape(heads, 1, 1), norm_g.reshape(1, dh))


def _cexp(re, im):
    m = jnp.exp(re)
    return m * jnp.cos(im), m * jnp.sin(im)


def _cmul(ar, ai, br, bi):
    return ar * br - ai * bi, ar * bi + ai * br


def _gelu_tanh(x):
    return 0.5 * x * (1.0 + jnp.tanh(math.sqrt(2.0 / math.pi) * (x + 0.044715 * (x * x * x))))


S5_GB = LANES // S5_GROUP


def _dot_hi(a, b):
    a0 = a.astype(BF16)
    a1 = (a - a0.astype(F32)).astype(BF16)
    b0 = b.astype(BF16)
    b1 = (b - b0.astype(F32)).astype(BF16)
    return _dot(a0, b0) + (_dot(a0, b1) + _dot(a1, b0))


def _s5_kernel(u_ref, bre_ref, bim_ref, cre_ref, cim_ref, lr_r_ref, li_r_ref, dt_r_ref,
               lr_c_ref, li_c_ref, dt_c_ref, d_ref, o_ref, ua_s, bps_s, cq_s, ks_s, y_s, *, nc):
    step, grp, st = S5_STEP, S5_GROUP, S5_STATE
    gw = S5_GB * grp
    half = S5_GB * st
    sw = 2 * half
    r = ua_s.shape[0]
    gsh, ssh = grp.bit_length() - 1, st.bit_length() - 1

    lr, li = lr_r_ref[0], li_r_ref[0]
    dt = jnp.exp(dt_r_ref[0])
    ar, ai = _cexp(lr * dt, li * dt)
    den = lr * lr + li * li
    pr = ((ar - 1.0) * lr + ai * li) / den
    pi_ = (ai * lr - (ar - 1.0) * li) / den
    re_lane = lax.broadcasted_iota(jnp.int32, (1, sw), 1) < half
    row_g = lax.shift_right_logical(lax.broadcasted_iota(jnp.int32, (gw, sw), 0), gsh)
    lane_g = lax.shift_right_logical(lax.broadcasted_iota(jnp.int32, (gw, sw), 1) & (half - 1), ssh)
    same_g = row_g == lane_g
    br = jnp.where(same_g, jnp.concatenate([bre_ref[0]] * (sw // LANES), axis=1), 0.0)
    bi = jnp.where(same_g, jnp.concatenate([bim_ref[0]] * (sw // LANES), axis=1), 0.0)
    bp0 = None
    for k in range(step):
        blk = br * jnp.where(re_lane, pr, pi_) + bi * jnp.where(re_lane, -pi_, pr)
        if k == 0:
            bp0 = blk
        s = step - 1 - k
        bps_s[s * gw:(s + 1) * gw, :] = blk.astype(BF16)
        pr, pi_ = _cmul(pr, pi_, ar, ai)

    dtc = jnp.exp(dt_c_ref[0])
    acr, aci = _cexp(lr_c_ref[0] * dtc, li_c_ref[0] * dtc)
    acr = jnp.broadcast_to(acr, (sw, gw))
    aci = jnp.broadcast_to(aci, (sw, gw))
    rrow = lax.broadcasted_iota(jnp.int32, (sw, gw), 0)
    re_row = rrow < half
    same_c = (lax.shift_right_logical(rrow & (half - 1), ssh)
              == lax.shift_right_logical(lax.broadcasted_iota(jnp.int32, (sw, gw), 1), gsh))
    pcr = jnp.where(same_c, cre_ref[0], 0.0)
    pci = jnp.where(same_c, cim_ref[0], 0.0)
    for k in range(step + 1):
        cqk = jnp.where(re_row, pcr, -pci)
        if k < step:
            ks_s[(step - 1 - k) * gw:(step - k) * gw, :] = _dot_hi(bp0, cqk).astype(BF16)
            pcr, pci = _cmul(pcr, pci, acr, aci)
        if k >= 1:
            cq_s[k - 1] = cqk.astype(BF16)

    for s in range(step):
        ua_s[:, s * gw:(s + 1) * gw] = u_ref[pl.ds(s, r, stride=step), :].astype(BF16)
    x = _dot(ua_s[...], bps_s[...])
    blk_i = lax.broadcasted_iota(jnp.int32, (r, sw), 0) & (nc - 1)
    er, ei = ar, ai
    for _ in range(step.bit_length() - 1):
        er, ei = _cmul(er, ei, er, ei)
    sh = 1
    while sh < nc:
        xs = jnp.where(blk_i >= sh, pltpu.roll(x, sh, axis=0), 0.0)
        x = x + xs * er + pltpu.roll(xs, half, axis=1) * jnp.where(re_lane, -ei, ei)
        er, ei = _cmul(er, ei, er, ei)
        sh *= 2
    xp = jnp.where(blk_i >= 1, pltpu.roll(x, 1, axis=0), 0.0).astype(BF16)

    d = d_ref[0]
    for t in range(step):
        y = (_dot(ua_s[:, :(t + 1) * gw], ks_s[(step - 1 - t) * gw:, :]) + _dot(xp, cq_s[t])
             + d * u_ref[pl.ds(t, r, stride=step), :])
        y_s[pl.ds(t, r, stride=step), :] = _gelu_tanh(y)

    rt = 512

    def emit(i, carry):
        r0 = pl.multiple_of(i * rt, rt)
        o_ref[pl.ds(r0, rt), :] = y_s[pl.ds(r0, rt), :].astype(o_ref.dtype)
        return carry

    lax.fori_loop(0, y_s.shape[0] // rt, emit, 0)


def _s5(u2, a_re, a_im, b_re, b_im, c_re, c_im, d, log_dt, seq):
    groups, st = a_re.shape
    grp, step = S5_GROUP, S5_STEP
    m = u2.shape[0]
    g8 = groups // S5_GB
    gw, half = S5_GB * grp, S5_GB * st
    sw = 2 * half
    nc = seq // step
    assert nc & (nc - 1) == 0 and gw == LANES
    r = m // step
    bt = lambda t: jnp.tile(t.transpose(0, 2, 1).reshape(g8, gw, st), (1, 1, 2))
    ct = lambda t: jnp.tile(t.transpose(0, 2, 1).reshape(g8, half, grp), (1, 2, S5_GB))
    rowv = lambda t: jnp.tile(t.reshape(g8, 1, half), (1, 1, 2))
    colv = lambda t: jnp.tile(t.reshape(g8, half, 1), (1, 2, 1))
    ldt = jnp.broadcast_to(log_dt[:, None], (groups, st))
    g3 = lambda shp: pl.BlockSpec((1,) + shp, lambda g: (g, 0, 0))
    return pl.pallas_call(
        functools.partial(_s5_kernel, nc=nc),
        grid=(g8,),
        in_specs=[pl.BlockSpec((m, gw), lambda g: (0, g)),
                  g3((gw, 2 * st)), g3((gw, 2 * st)), g3((sw, gw)), g3((sw, gw)),
                  g3((1, sw)), g3((1, sw)), g3((1, sw)), g3((sw, 1)), g3((sw, 1)), g3((sw, 1)),
                  g3((1, gw))],
        out_specs=pl.BlockSpec((m, gw), lambda g: (0, g)),
        out_shape=jax.ShapeDtypeStruct((m, groups * grp), BF16),
        scratch_shapes=[pltpu.VMEM((r, step * gw), BF16), pltpu.VMEM((step * gw, sw), BF16),
                        pltpu.VMEM((step, sw, gw), BF16), pltpu.VMEM((step * gw, gw), BF16),
                        pltpu.VMEM((m, gw), F32)],
        compiler_params=_params("arbitrary"),
        name="s5_ssm",
    )(u2, bt(b_re), bt(b_im), ct(c_re), ct(c_im), rowv(a_re), rowv(a_im), rowv(ldt),
      colv(a_re), colv(a_im), colv(ldt), d.reshape(g8, 1, gw))


def _sconv_kernel(gb_ref, gc_ref, xc_ref, w_ref, o_ref):
    p = gc_ref[...].astype(F32) * xc_ref[...].astype(F32)
    w = w_ref[...]
    row = lax.broadcasted_iota(jnp.int32, p.shape, 0)
    acc = p * w[C_CONV - 1:C_CONV, :]
    for s in range(1, C_CONV):
        acc = acc + _shift_rows(p, s, row) * w[C_CONV - 1 - s:C_CONV - s, :]
    o_ref[...] = (gb_ref[...].astype(F32) * acc).astype(o_ref.dtype)


def _sconv(proj, conv_w, bsz, seq):
    width = conv_w.shape[1]
    tc = 256
    nb = width // tc
    col = lambda off: pl.BlockSpec((seq, tc), lambda b, j, off=off: (b, off * nb + j))
    return pl.pallas_call(
        _sconv_kernel,
        grid=(bsz, nb),
        in_specs=[col(0), col(1), col(2), pl.BlockSpec((C_CONV, tc), lambda b, j: (0, j))],
        out_specs=pl.BlockSpec((seq, tc), lambda b, j: (b, j)),
        out_shape=jax.ShapeDtypeStruct((bsz * seq, width), BF16),
        compiler_params=_params("arbitrary", "arbitrary"),
        name="short_conv",
    )(proj, proj, proj, conv_w)


NEG_BIG = -1e30


def _dattn_kernel(q_ref, k_ref, v_ref, lq1_ref, lk1_ref, lq2_ref, lk2_ref, ng_ref, o_ref,
                  acc_s, *, heads, tq, lambda_init):
    dh = DA_HEAD_DIM
    qi = pl.program_id(1)
    scale = dh ** -0.5
    lam = (jnp.exp(jnp.sum(lq1_ref[...] * lk1_ref[...], axis=-1, keepdims=True))
           - jnp.exp(jnp.sum(lq2_ref[...] * lk2_ref[...], axis=-1, keepdims=True)) + lambda_init)
    ri = lax.broadcasted_iota(jnp.int32, (tq, tq), 0)
    ci = lax.broadcasted_iota(jnp.int32, (tq, tq), 1)
    rel = (ri - ci).astype(F32)
    causal = ri >= ci
    ng = ng_ref[...]

    for h in range(heads):
        slope = 2.0 ** (-8.0 * (h + 1) / heads)
        bias0 = -slope * rel
        v_lo = h * 2 * dh
        outs = []
        for mp in range(2):
            q_lo = v_lo + mp * dh
            qm = q_ref[:, q_lo:q_lo + dh]

            def block(j, carry, masked, qm=qm, q_lo=q_lo):
                m, l = carry
                k0 = pl.multiple_of(j * tq, tq)
                km = k_ref[pl.ds(k0, tq), q_lo:q_lo + dh]
                vm = v_ref[pl.ds(k0, tq), v_lo:v_lo + 2 * dh]
                s = _dot_nt(qm, km) * scale + (bias0 - slope * ((qi - j) * tq).astype(F32))
                if masked:
                    s = jnp.where(causal, s, NEG_BIG)
                m_new = jnp.maximum(m, jnp.max(s, axis=-1, keepdims=True))
                corr = jnp.exp(m - m_new)
                p = jnp.exp(s - m_new)
                l = corr * l + jnp.sum(p, axis=-1, keepdims=True)
                acc_s[mp] = corr * acc_s[mp] + _dot(p.astype(BF16), vm)
                return m_new, l

            acc_s[mp] = jnp.zeros((tq, 2 * dh), F32)
            carry = (jnp.full((tq, 1), NEG_BIG, F32), jnp.zeros((tq, 1), F32))
            carry = lax.fori_loop(0, qi, functools.partial(block, masked=False), carry)
            m, l = block(qi, carry, True)
            outs.append(acc_s[mp] / l)
        o = outs[0] - lam * outs[1]
        on = o * lax.rsqrt(jnp.mean(o * o, axis=-1, keepdims=True) + EPS) * ng * (1.0 - lambda_init)
        o_ref[:, v_lo:v_lo + 2 * dh] = on.astype(o_ref.dtype)


def _dattn(proj, lq1, lk1, lq2, lk2, norm_g, lambda_init, bsz, seq):
    dh = DA_HEAD_DIM
    width = proj.shape[1] // 6
    heads = width // (2 * dh)
    tq = min(256, seq)
    nq = seq // tq
    vec = pl.BlockSpec((1, dh), lambda b, i: (0, 0))
    return pl.pallas_call(
        functools.partial(_dattn_kernel, heads=heads, tq=tq, lambda_init=lambda_init),
        grid=(bsz, nq),
        in_specs=[pl.BlockSpec((tq, width), lambda b, i: (b * nq + i, 3)),
                  pl.BlockSpec((seq, width), lambda b, i: (b, 4)),
                  pl.BlockSpec((seq, width), lambda b, i: (b, 5)),
                  vec, vec, vec, vec, pl.BlockSpec((1, 2 * dh), lambda b, i: (0, 0))],
        out_specs=pl.BlockSpec((tq, width), lambda b, i: (b * nq + i, 0)),
        out_shape=jax.ShapeDtypeStruct((bsz * seq, width), BF16),
        scratch_shapes=[pltpu.VMEM((2, tq, 2 * dh), F32)],
        compiler_params=_params("arbitrary", "arbitrary"),
        name="diff_attention",
    )(proj, proj, proj, lq1.reshape(1, dh), lk1.reshape(1, dh), lq2.reshape(1, dh), lk2.reshape(1, dh),
      norm_g.reshape(1, 2 * dh))


def kernel(x, c, ada_w, ada_b, ln1_g, ln1_b, ln2_g, ln2_b, ffn_w_gate, ffn_w_up, ffn_w_down, ab_w_in, ab_w_out, dn_conv_w, dn_a_log, dn_dt_bias, dn_norm_g, s5_a_re, s5_a_im, s5_b_re, s5_b_im, s5_c_re, s5_c_im, s5_d, s5_log_dt, s5_w_glu, cd_w_in, cd_w_out, sc_conv_w, da_lq1, da_lk1, da_lq2, da_lk2, da_norm_g):
    bsz, seq, d = x.shape
    depth = ada_w.shape[0]
    m = bsz * seq
    a_width = dn_conv_w.shape[-1] // 3
    heads = dn_a_log.shape[-1]

    c_pad = jnp.zeros((8, d), F32).at[:bsz].set(c)
    mod = _ada_mod(c_pad, ada_w, ada_b)

    def mvec(i, j):
        return mod[i, :bsz, j * d:(j + 1) * d].reshape(bsz, 1, d)

    x2 = x.reshape(m, d)
    h = _modulate(x2, mvec(0, 1), mvec(0, 0), seq)
    for i in range(depth):
        j = i // 2
        if i % 2 == 0:
            w_u = ab_w_in[j:j + 1, :, 4 * a_width + 2 * heads:]
            w_ba = jnp.pad(ab_w_in[j:j + 1, :, 4 * a_width:4 * a_width + 2 * heads],
                           ((0, 0), (0, 0), (0, LANES - 2 * heads)))
            proj = _mm(h, ab_w_in, j, 4 * a_width, BF16, 1024, 1024, "ab_in_proj")
            uf = _mm(h, w_u, 0, w_u.shape[2], F32, 1024, 1024, "ab_u_proj")
            ba = _mm(h, w_ba, 0, LANES, F32, 1024, LANES, "ab_gate_proj")
            ya = _deltanet(proj, ba, dn_conv_w[j], dn_a_log[j], dn_dt_bias[j], dn_norm_g[j], bsz, seq)
            yb = _s5(uf, s5_a_re[j], s5_a_im[j], s5_b_re[j], s5_b_im[j],
                     s5_c_re[j], s5_c_im[j], s5_d[j], s5_log_dt[j], seq)
            yb = _glu(yb, s5_w_glu[j].astype(BF16), 1024)
            w_out = ab_w_out[j].astype(BF16)
        else:
            lambda_init = 0.8 - 0.6 * math.exp(-0.3 * i)
            proj = _mm(h, cd_w_in, j, cd_w_in.shape[2], BF16, 1024, 1024, "cd_in_proj")
            ya = _sconv(proj, sc_conv_w[j], bsz, seq)
            yb = _dattn(proj, da_lq1[j], da_lk1[j], da_lq2[j], da_lk2[j], da_norm_g[j], lambda_init, bsz, seq)
            w_out = cd_w_out[j].astype(BF16)
        x2, h = _res_ln([ya, yb], w_out, x2, mvec(i, 2), ln1_g[i], ln1_b[i],
                        (mvec(i, 4), mvec(i, 3)), seq, 512, ya.shape[1], "mixer_out_ln")
        act = _gate_up(h, ffn_w_gate, ffn_w_up, i, 1024, 512)
        nxt = (mvec(i + 1, 1), mvec(i + 1, 0)) if i + 1 < depth else None
        x2, h = _res_ln([act], ffn_w_down[i].astype(BF16), x2, mvec(i, 5), ln2_g[i], ln2_b[i],
                        nxt, seq, 512, 1408, "ffn_down_ln")
    return x2.reshape(bsz, seq, d)
```

```python
import functools
import math

import jax
import jax.numpy as jnp
from jax import lax
from jax.experimental import pallas as pl
from jax.experimental.pallas import tpu as pltpu

F32 = jnp.float32
BF16 = jnp.bfloat16

DEPTH = 4
A_HEAD_DIM = 128
A_CONV = 4
CHUNK = 64
S5_GROUP = 16
S5_STATE = 64
S5_STEP = 16
C_CONV = 3
DA_HEAD_DIM = 128
ALPHA = (2.0 * DEPTH) ** 0.25
EPS = 1e-5
LANES = 128
VMEM_LIMIT_BYTES = 56 * 1024 * 1024


def _params(*sem):
    return pltpu.CompilerParams(dimension_semantics=sem, vmem_limit_bytes=VMEM_LIMIT_BYTES)


def _silu(x):
    return x * jax.nn.sigmoid(x)


def _dot(a, b):
    return jnp.dot(a, b, preferred_element_type=F32)


def _dot_nt(a, b):
    return lax.dot_general(a, b, (((1,), (1,)), ((), ())), preferred_element_type=F32)


def _dot_tn(a, b):
    return lax.dot_general(a, b, (((0,), (0,)), ((), ())), preferred_element_type=F32)


def _split3(x):
    hi = x.astype(BF16)
    r1 = x - hi.astype(F32)
    mid = r1.astype(BF16)
    lo = (r1 - mid.astype(F32)).astype(BF16)
    return hi, mid, lo


def _dot_f32(a, b):
    a0, a1, a2 = _split3(a)
    b0, b1, b2 = _split3(b)
    return (_dot(a0, b0) + (_dot(a0, b1) + _dot(a1, b0))
            + (_dot(a1, b1) + _dot(a0, b2) + _dot(a2, b0)))


def _ada_kernel(c_ref, w_ref, b_ref, o_ref):
    c = c_ref[...]
    ca = _silu(c).astype(BF16)
    o_ref[0] = _dot(ca, w_ref[0].astype(BF16)) + b_ref[0]


def _ada_mod(c_pad, ada_w, ada_b):
    depth, d, n = ada_w.shape
    rows = c_pad.shape[0]
    tn = 1024
    return pl.pallas_call(
        _ada_kernel,
        grid=(depth, n // tn),
        in_specs=[pl.BlockSpec((rows, d), lambda i, j: (0, 0)),
                  pl.BlockSpec((1, d, tn), lambda i, j: (i, 0, j)),
                  pl.BlockSpec((1, 1, tn), lambda i, j: (i, 0, j))],
        out_specs=pl.BlockSpec((1, rows, tn), lambda i, j: (i, 0, j)),
        out_shape=jax.ShapeDtypeStruct((depth, rows, n), F32),
        compiler_params=_params("arbitrary", "arbitrary"),
        name="ada_mod",
    )(c_pad, ada_w, ada_b.reshape(depth, 1, n))


def _modulate_kernel(x_ref, sc_ref, sh_ref, o_ref):
    o_ref[...] = (x_ref[...] * (1.0 + sc_ref[0]) + sh_ref[0]).astype(o_ref.dtype)


def _modulate(x2, sc, sh, seq):
    m, d = x2.shape
    tl = min(seq, 512)
    nl = seq // tl
    return pl.pallas_call(
        _modulate_kernel,
        grid=(m // tl,),
        in_specs=[pl.BlockSpec((tl, d), lambda i: (i, 0)),
                  pl.BlockSpec((1, 1, d), lambda i: (i // nl, 0, 0)),
                  pl.BlockSpec((1, 1, d), lambda i: (i // nl, 0, 0))],
        out_specs=pl.BlockSpec((tl, d), lambda i: (i, 0)),
        out_shape=jax.ShapeDtypeStruct((m, d), BF16),
        compiler_params=_params("arbitrary"),
        name="modulate",
    )(x2, sc, sh)


def _mm_kernel(a_ref, w_ref, o_ref, w_s):
    @pl.when(pl.program_id(1) == 0)
    def _():
        w_s[...] = w_ref[0].astype(BF16)

    o_ref[...] = _dot(a_ref[...], w_s[...]).astype(o_ref.dtype)


def _mm(a, w, layer, n, out_dtype, tm, tn, name):
    m, k = a.shape
    tm = min(tm, m)
    tn = min(tn, n)
    return pl.pallas_call(
        _mm_kernel,
        grid=(n // tn, m // tm),
        in_specs=[pl.BlockSpec((tm, k), lambda j, i: (i, 0)),
                  pl.BlockSpec((1, k, tn), lambda j, i: (layer, 0, j))],
        out_specs=pl.BlockSpec((tm, tn), lambda j, i: (i, j)),
        out_shape=jax.ShapeDtypeStruct((m, n), out_dtype),
        scratch_shapes=[pltpu.VMEM((k, tn), BF16)],
        compiler_params=_params("arbitrary", "arbitrary"),
        name=name,
    )(a, w)


def _mm_t_kernel(wt_ref, a_ref, o_ref, w_s):
    @pl.when(pl.program_id(1) == 0)
    def _():
        w_s[...] = wt_ref[...].astype(BF16)

    o_ref[0] = _dot_nt(w_s[...], a_ref[...]).astype(o_ref.dtype)


def _mm_t(a, wt, bsz, seq, tn, tl, name):
    m, k = a.shape
    n = wt.shape[0]
    tl = min(tl, seq)
    nl = seq // tl
    return pl.pallas_call(
        _mm_t_kernel,
        grid=(n // tn, m // tl),
        in_specs=[pl.BlockSpec((tn, k), lambda j, i: (j, 0)),
                  pl.BlockSpec((tl, k), lambda j, i: (i, 0))],
        out_specs=pl.BlockSpec((1, tn, tl), lambda j, i: (i // nl, j, i % nl)),
        out_shape=jax.ShapeDtypeStruct((bsz, n, seq), BF16),
        scratch_shapes=[pltpu.VMEM((tn, k), BF16)],
        compiler_params=_params("arbitrary", "arbitrary"),
        name=name,
    )(wt, a)


def _gate_up_kernel(a_ref, wg_ref, wu_ref, o_ref, wg_s, wu_s):
    @pl.when(pl.program_id(1) == 0)
    def _():
        wg_s[...] = wg_ref[0].astype(BF16)
        wu_s[...] = wu_ref[0].astype(BF16)

    a = a_ref[...]
    g = _dot(a, wg_s[...])
    u = _dot(a, wu_s[...])
    o_ref[...] = (_silu(g) * u).astype(o_ref.dtype)


def _gate_up(a, wg, wu, layer, tm, tn):
    m, k = a.shape
    n = wg.shape[2]
    tm = min(tm, m)
    return pl.pallas_call(
        _gate_up_kernel,
        grid=(n // tn, m // tm),
        in_specs=[pl.BlockSpec((tm, k), lambda j, i: (i, 0)),
                  pl.BlockSpec((1, k, tn), lambda j, i: (layer, 0, j)),
                  pl.BlockSpec((1, k, tn), lambda j, i: (layer, 0, j))],
        out_specs=pl.BlockSpec((tm, tn), lambda j, i: (i, j)),
        out_shape=jax.ShapeDtypeStruct((m, n), BF16),
        scratch_shapes=[pltpu.VMEM((k, tn), BF16), pltpu.VMEM((k, tn), BF16)],
        compiler_params=_params("arbitrary", "arbitrary"),
        name="ffn_gate_up",
    )(a, wg, wu)


def _glu_kernel(y_ref, w_ref, o_ref):
    y = y_ref[...]
    t = _dot(y, w_ref[...])
    o_ref[...] = (y.astype(F32) * jax.nn.sigmoid(t)).astype(o_ref.dtype)


def _glu(y, w, tm):
    m, k = y.shape
    tm = min(tm, m)
    return pl.pallas_call(
        _glu_kernel,
        grid=(m // tm,),
        in_specs=[pl.BlockSpec((tm, k), lambda i: (i, 0)),
                  pl.BlockSpec((k, k), lambda i: (0, 0))],
        out_specs=pl.BlockSpec((tm, k), lambda i: (i, 0)),
        out_shape=jax.ShapeDtypeStruct((m, k), BF16),
        compiler_params=_params("arbitrary"),
        name="s5_glu",
    )(y, w)


def _res_ln_kernel(*refs, n_in, has_next):
    a_refs = refs[:n_in]
    w_refs = refs[n_in:2 * n_in]
    x_ref, g_ref, lng_ref, lnb_ref = refs[2 * n_in:2 * n_in + 4]
    pos = 2 * n_in + 4
    if has_next:
        sc_ref, sh_ref = refs[pos:pos + 2]
        pos += 2
    xo_ref = refs[pos]
    pos += 1
    if has_next:
        ho_ref = refs[pos]
        pos += 1
    y_bufs = refs[pos:pos + 2]
    i = pl.program_id(0)

    @pl.when(i == 0)
    def _():
        y_bufs[1][...] = jnp.zeros(y_bufs[1].shape, F32)

    def step(y_new, y_old):
        part = _dot(a_refs[0][...], w_refs[0][...])
        for a_ref, w_ref in zip(a_refs[1:], w_refs[1:]):
            part = part + _dot(a_ref[...], w_ref[...])
        y_new[...] = part
        r = ALPHA * x_ref[...] + (1.0 + g_ref[0]) * y_old[...]
        mu = jnp.mean(r, axis=-1, keepdims=True)
        rc = r - mu
        var = jnp.mean(rc * rc, axis=-1, keepdims=True)
        xn = rc * lax.rsqrt(var + EPS) * lng_ref[...] + lnb_ref[...]
        xo_ref[...] = xn
        if has_next:
            ho_ref[...] = (xn * (1.0 + sc_ref[0]) + sh_ref[0]).astype(ho_ref.dtype)

    even = lax.rem(i, 2) == 0

    @pl.when(even)
    def _():
        step(y_bufs[0], y_bufs[1])

    @pl.when(jnp.logical_not(even))
    def _():
        step(y_bufs[1], y_bufs[0])


def _res_ln(a_list, w, x2, gate, ln_g, ln_b, nxt, seq, tm, name):
    m, d = x2.shape
    n_in = len(a_list)
    ka = a_list[0].shape[1]
    tm = min(tm, m, seq)
    nt = m // tm
    nl = seq // tm
    has_next = nxt is not None
    cur = lambda i: (jnp.minimum(i, nt - 1), 0)
    prev = lambda i: (jnp.maximum(i - 1, 0), 0)
    in_specs = [pl.BlockSpec((tm, ka), cur) for _ in a_list]
    in_specs += [pl.BlockSpec((ka, d), functools.partial(lambda i, s: (s, 0), s=s), pipeline_mode=pl.Buffered(1))
                 for s in range(n_in)]
    bvec = pl.BlockSpec((1, 1, d), lambda i: (jnp.maximum(i - 1, 0) // nl, 0, 0))
    pvec = pl.BlockSpec((1, d), lambda i: (0, 0))
    in_specs += [pl.BlockSpec((tm, d), prev), bvec, pvec, pvec]
    args = list(a_list) + [w] * n_in + [x2, gate, ln_g.reshape(1, d), ln_b.reshape(1, d)]
    out_shape = [jax.ShapeDtypeStruct((m, d), F32)]
    out_specs = [pl.BlockSpec((tm, d), prev)]
    if has_next:
        in_specs += [bvec, bvec]
        args += list(nxt)
        out_shape.append(jax.ShapeDtypeStruct((m, d), BF16))
        out_specs.append(pl.BlockSpec((tm, d), prev))
    out = pl.pallas_call(
        functools.partial(_res_ln_kernel, n_in=n_in, has_next=has_next),
        grid=(nt + 1,),
        in_specs=in_specs,
        out_specs=out_specs,
        out_shape=out_shape,
        scratch_shapes=[pltpu.VMEM((tm, d), F32), pltpu.VMEM((tm, d), F32)],
        compiler_params=_params("arbitrary"),
        name=name,
    )(*args)
    return (out[0], out[1]) if has_next else (out[0], None)


def _shift_rows(x, s, row):
    return jnp.where(row >= s, pltpu.roll(x, s, axis=0), 0.0)


def _conv_silu(x_ref, cw_ref):
    x = x_ref[...].astype(F32)
    w = cw_ref[...]
    row = lax.broadcasted_iota(jnp.int32, x.shape, 0)
    acc = x * w[A_CONV - 1:A_CONV, :]
    for s in range(1, A_CONV):
        acc = acc + _shift_rows(x, s, row) * w[A_CONV - 1 - s:A_CONV - s, :]
    return _silu(acc)


def _l2n(t):
    return t * lax.rsqrt(jnp.sum(t * t, axis=-1, keepdims=True) + 1e-6)


DN_HB = 4


def _delta_kernel(q_ref, k_ref, v_ref, z_ref, cwq_ref, cwk_ref, cwv_ref, a_ref, b_ref,
                  alog_ref, dtb_ref, ng_ref, o_ref,
                  q_s, k_s, v_s, gc_s, bt_s, u_s, wq_s, l2_s, gl_s, s_s, n_s, nt_s, x_s, *, nc, seq):
    c_len, dh = CHUNK, A_HEAD_DIM
    nmat_all = DN_HB * nc
    hh = lax.rem(pl.program_id(1), DN_HB)
    q_s[hh] = _l2n(_conv_silu(q_ref, cwq_ref)) * (dh ** -0.5)
    k_s[hh] = _l2n(_conv_silu(k_ref, cwk_ref))
    v_s[hh] = _conv_silu(v_ref, cwv_ref)

    a = a_ref[0, 0]
    b = b_ref[0, 0]
    sp_in = a + dtb_ref[0]
    softplus = jnp.maximum(sp_in, 0.0) + jnp.log1p(jnp.exp(-jnp.abs(sp_in)))
    g = -jnp.exp(alog_ref[0]) * softplus
    ri = lax.broadcasted_iota(jnp.int32, (c_len, LANES), 0)
    ci = lax.broadcasted_iota(jnp.int32, (c_len, LANES), 1)
    tril, strict, eye = ri >= ci, ri > ci, ri == ci
    eye64 = eye[:, :c_len]
    eye_bf = jnp.where(eye64, 1.0, 0.0).astype(BF16)
    g0, g1, g2 = _split3(g)
    triu = jnp.where(ci < c_len, jnp.where(ri <= ci, 1.0, 0.0), 0.0).astype(BF16)
    gc_s[hh] = _dot(g0, triu) + _dot(g1, triu) + _dot(g2, triu)
    bt_s[hh] = jax.nn.sigmoid(b)

    def prep(c, carry):
        r0 = pl.multiple_of(c * c_len, c_len)
        kc = k_s[hh, pl.ds(r0, c_len), :]
        qc = q_s[hh, pl.ds(r0, c_len), :]
        grow = gc_s[hh, pl.ds(c, 1), :]
        brow = bt_s[hh, pl.ds(c, 1), :]
        gcol = jnp.sum(jnp.where(eye, grow, 0.0), axis=1, keepdims=True)
        bcol = jnp.sum(jnp.where(eye64, brow, 0.0), axis=1, keepdims=True)
        decay = jnp.where(tril, jnp.exp(jnp.where(tril, gcol - grow, 0.0)), 0.0)
        kb = kc.astype(BF16)
        kk = _dot_nt(kb, jnp.concatenate([kb, jnp.zeros_like(kb)], axis=0))
        m0 = pl.multiple_of((hh * nc + c) * c_len, c_len)
        n_s[pl.ds(m0, c_len), :] = jnp.where(strict, kk * bcol * decay, 0.0)
        qk = _dot_nt(qc.astype(BF16), kb)
        glast = grow[:, c_len - 1:c_len]
        kt = (kc * jnp.exp(glast - gcol)).astype(BF16)
        wq_s[hh * nc + c, c_len:, :] = (qc * jnp.exp(gcol)).astype(BF16)
        l2_s[hh * nc + c, :c_len, :] = (qk * decay[:, :c_len]).astype(BF16)
        l2_s[hh * nc + c, c_len:, :] = _dot_tn(kt, eye_bf).astype(BF16)
        gl_s[hh, pl.ds(c, 1), :] = jnp.broadcast_to(jnp.exp(glast), (1, LANES))
        return carry

    lax.fori_loop(0, nc, prep, 0, unroll=2)

    @pl.when(hh == DN_HB - 1)
    def _():
        x_s[...] = jnp.zeros(x_s.shape, F32)
        for i in range(c_len):
            slab = n_s[pl.ds(i, nmat_all, stride=c_len), :]
            nt_s[i] = slab.T[:c_len, :]
        sub = lax.broadcasted_iota(jnp.int32, (8, LANES), 0)
        for ib in range(c_len // 8):
            npc = ib + 1

            def row(ii, carry, ib=ib, npc=npc):
                i = ib * 8 + ii
                parts = [[None] * 4 for _ in range(npc)]
                for j in range(8 * npc):
                    coef = nt_s[i, pl.ds(j, 1), :]
                    for p in range(j // 8 + 1):
                        term = coef * x_s[j, p * 8:(p + 1) * 8, :]
                        k = j % 4
                        parts[p][k] = term if parts[p][k] is None else parts[p][k] + term
                for p in range(npc):
                    live = [t for t in parts[p] if t is not None]
                    tot = live[0]
                    for t in live[1:]:
                        tot = tot + t
                    x_s[i, p * 8:(p + 1) * 8, :] = jnp.where(sub + p * 8 == i, 1.0, 0.0) - tot
                return carry

            lax.fori_loop(0, 8, row, 0)
        zpad = jnp.zeros((LANES - c_len, LANES), F32)
        for i in range(c_len):
            n_s[pl.ds(i, nmat_all, stride=c_len), :] = jnp.concatenate([x_s[i], zpad], axis=0).T

        nshift = nc.bit_length() - 1

        def uw(idx, carry):
            hd = lax.shift_right_logical(idx, nshift)
            c = idx & (nc - 1)
            r0 = pl.multiple_of(c * c_len, c_len)
            t = n_s[pl.ds(pl.multiple_of(idx * c_len, c_len), c_len), :][:, :c_len]
            tb = t * bt_s[hd, pl.ds(c, 1), :]
            tw = tb * jnp.exp(gc_s[hd, pl.ds(c, 1), :][:, :c_len])
            kb = k_s[hd, pl.ds(r0, c_len), :].astype(BF16)
            vb = v_s[hd, pl.ds(r0, c_len), :].astype(BF16)
            u_s[hd, pl.ds(r0, c_len), :] = _dot(tb.astype(BF16), vb)
            wq_s[idx, :c_len, :] = _dot(tw.astype(BF16), kb).astype(BF16)
            return carry

        lax.fori_loop(0, nmat_all, uw, 0, unroll=2)

        ng = ng_ref[...]
        s_s[...] = jnp.zeros(s_s.shape, F32)

        def scan(c, carry):
            r0 = pl.multiple_of(c * c_len, c_len)
            for hd in range(DN_HB):
                s = s_s[hd]
                sb = s.astype(BF16)
                r1 = _dot(wq_s[hd * nc + c], sb)
                v_new = u_s[hd, pl.ds(r0, c_len), :] - r1[:c_len]
                r2 = _dot(l2_s[hd * nc + c], v_new.astype(BF16))
                o = r1[c_len:] + r2[:c_len]
                s_s[hd] = s * gl_s[hd, pl.ds(c, 1), :] + r2[c_len:]
                u_s[hd, pl.ds(r0, c_len), :] = o * lax.rsqrt(jnp.mean(o * o, axis=-1, keepdims=True) + EPS) * ng
            return carry

        lax.fori_loop(0, nc, scan, 0)

        rt = 256

        def gate(r, carry):
            r0 = pl.multiple_of(r * rt, rt)
            for hd in range(DN_HB):
                zc = z_ref[pl.ds(r0, rt), hd * dh:(hd + 1) * dh].astype(F32)
                o_ref[pl.ds(r0, rt), hd * dh:(hd + 1) * dh] = (u_s[hd, pl.ds(r0, rt), :] * _silu(zc)).astype(o_ref.dtype)
            return carry

        lax.fori_loop(0, seq // rt, gate, 0)


def _deltanet(proj, ba, conv_w, a_log, dt_bias, norm_g, bsz, seq):
    dh = A_HEAD_DIM
    heads = a_log.shape[0]
    nc = seq // CHUNK
    assert DN_HB * nc == LANES and heads % DN_HB == 0
    bh = ba[:, :2 * heads].reshape(bsz, seq, 2, heads).transpose(2, 0, 3, 1).reshape(2, bsz, heads, nc, CHUNK)
    col = lambda off: pl.BlockSpec((seq, dh), lambda b, h, off=off: (b, off + h))
    cw = lambda off: pl.BlockSpec((A_CONV, dh), lambda b, h, off=off: (0, off + h))
    rows = pl.BlockSpec((1, 1, nc, CHUNK), lambda b, h: (b, h, 0, 0))
    scal = pl.BlockSpec((1, 1, 1), lambda b, h: (h, 0, 0))
    hgroups = heads // DN_HB
    return pl.pallas_call(
        functools.partial(_delta_kernel, nc=nc, seq=seq),
        grid=(bsz, heads),
        in_specs=[col(0), col(heads), col(2 * heads),
                  pl.BlockSpec((seq, DN_HB * dh), lambda b, h: (b, 3 * hgroups + h // DN_HB)),
                  cw(0), cw(heads), cw(2 * heads), rows, rows, scal, scal,
                  pl.BlockSpec((1, dh), lambda b, h: (0, 0))],
        out_specs=pl.BlockSpec((seq, DN_HB * dh), lambda b, h: (b, h // DN_HB)),
        out_shape=jax.ShapeDtypeStruct((bsz * seq, heads * dh), BF16),
        scratch_shapes=[pltpu.VMEM((DN_HB, seq, dh), F32), pltpu.VMEM((DN_HB, seq, dh), F32),
                        pltpu.VMEM((DN_HB, seq, dh), F32),
                        pltpu.VMEM((DN_HB, nc, LANES), F32), pltpu.VMEM((DN_HB, nc, CHUNK), F32),
                        pltpu.VMEM((DN_HB, seq, dh), F32),
                        pltpu.VMEM((DN_HB * nc, 2 * CHUNK, dh), BF16), pltpu.VMEM((DN_HB * nc, CHUNK + dh, CHUNK), BF16),
                        pltpu.VMEM((DN_HB, nc, LANES), F32), pltpu.VMEM((DN_HB, dh, dh), F32),
                        pltpu.VMEM((DN_HB * nc * CHUNK, LANES), F32), pltpu.VMEM((CHUNK, CHUNK, LANES), F32),
                        pltpu.VMEM((CHUNK, CHUNK, LANES), F32)],
        compiler_params=_params("arbitrary", "arbitrary"),
        name="gated_deltanet",
    )(proj, proj, proj, proj, conv_w, conv_w, conv_w, bh[1], bh[0],
      a_log.reshape(heads, 1, 1), dt_bias.reshape(heads, 1, 1), norm_g.reshape(1, dh))


def _cexp(re, im):
    m = jnp.exp(re)
    return m * jnp.cos(im), m * jnp.sin(im)


def _cmul(ar, ai, br, bi):
    return ar * br - ai * bi, ar * bi + ai * br


def _gelu_tanh(x):
    return 0.5 * x * (1.0 + jnp.tanh(math.sqrt(2.0 / math.pi) * (x + 0.044715 * (x * x * x))))


S5_GB = LANES // S5_GROUP


def _dot_hi(a, b):
    a0 = a.astype(BF16)
    a1 = (a - a0.astype(F32)).astype(BF16)
    b0 = b.astype(BF16)
    b1 = (b - b0.astype(F32)).astype(BF16)
    return _dot(a0, b0) + (_dot(a0, b1) + _dot(a1, b0))


def _s5_kernel(u_ref, bre_ref, bim_ref, cre_ref, cim_ref, lr_r_ref, li_r_ref, dt_r_ref,
               lr_c_ref, li_c_ref, dt_c_ref, d_ref, o_ref, ua_s, bps_s, cq_s, ks_s, y_s, *, nc):
    step, grp, st = S5_STEP, S5_GROUP, S5_STATE
    gw = S5_GB * grp
    half = S5_GB * st
    sw = 2 * half
    r = ua_s.shape[0]
    gsh, ssh = grp.bit_length() - 1, st.bit_length() - 1

    lr, li = lr_r_ref[0], li_r_ref[0]
    dt = jnp.exp(dt_r_ref[0])
    ar, ai = _cexp(lr * dt, li * dt)
    den = lr * lr + li * li
    pr = ((ar - 1.0) * lr + ai * li) / den
    pi_ = (ai * lr - (ar - 1.0) * li) / den
    re_lane = lax.broadcasted_iota(jnp.int32, (1, sw), 1) < half
    row_g = lax.shift_right_logical(lax.broadcasted_iota(jnp.int32, (gw, sw), 0), gsh)
    lane_g = lax.shift_right_logical(lax.broadcasted_iota(jnp.int32, (gw, sw), 1) & (half - 1), ssh)
    same_g = row_g == lane_g
    br = jnp.where(same_g, jnp.concatenate([bre_ref[0]] * (sw // LANES), axis=1), 0.0)
    bi = jnp.where(same_g, jnp.concatenate([bim_ref[0]] * (sw // LANES), axis=1), 0.0)
    bp0 = None
    for k in range(step):
        blk = br * jnp.where(re_lane, pr, pi_) + bi * jnp.where(re_lane, -pi_, pr)
        if k == 0:
            bp0 = blk
        s = step - 1 - k
        bps_s[s * gw:(s + 1) * gw, :] = blk.astype(BF16)
        pr, pi_ = _cmul(pr, pi_, ar, ai)

    dtc = jnp.exp(dt_c_ref[0])
    acr, aci = _cexp(lr_c_ref[0] * dtc, li_c_ref[0] * dtc)
    acr = jnp.broadcast_to(acr, (sw, gw))
    aci = jnp.broadcast_to(aci, (sw, gw))
    rrow = lax.broadcasted_iota(jnp.int32, (sw, gw), 0)
    re_row = rrow < half
    same_c = (lax.shift_right_logical(rrow & (half - 1), ssh)
              == lax.shift_right_logical(lax.broadcasted_iota(jnp.int32, (sw, gw), 1), gsh))
    pcr = jnp.where(same_c, cre_ref[0], 0.0)
    pci = jnp.where(same_c, cim_ref[0], 0.0)
    for k in range(step + 1):
        cqk = jnp.where(re_row, pcr, -pci)
        if k < step:
            ks_s[(step - 1 - k) * gw:(step - k) * gw, :] = _dot_hi(bp0, cqk).astype(BF16)
            pcr, pci = _cmul(pcr, pci, acr, aci)
        if k >= 1:
            cq_s[k - 1] = cqk.astype(BF16)

    for s in range(step):
        ua_s[:, s * gw:(s + 1) * gw] = u_ref[pl.ds(s, r, stride=step), :].astype(BF16)
    x = _dot(ua_s[...], bps_s[...])
    blk_i = lax.broadcasted_iota(jnp.int32, (r, sw), 0) & (nc - 1)
    er, ei = ar, ai
    for _ in range(step.bit_length() - 1):
        er, ei = _cmul(er, ei, er, ei)
    sh = 1
    while sh < nc:
        xs = jnp.where(blk_i >= sh, pltpu.roll(x, sh, axis=0), 0.0)
        x = x + xs * er + pltpu.roll(xs, half, axis=1) * jnp.where(re_lane, -ei, ei)
        er, ei = _cmul(er, ei, er, ei)
        sh *= 2
    xp = jnp.where(blk_i >= 1, pltpu.roll(x, 1, axis=0), 0.0).astype(BF16)

    d = d_ref[0]
    for t in range(step):
        y = (_dot(ua_s[:, :(t + 1) * gw], ks_s[(step - 1 - t) * gw:, :]) + _dot(xp, cq_s[t])
             + d * u_ref[pl.ds(t, r, stride=step), :])
        y_s[pl.ds(t, r, stride=step), :] = _gelu_tanh(y)

    rt = 512

    def emit(i, carry):
        r0 = pl.multiple_of(i * rt, rt)
        o_ref[pl.ds(r0, rt), :] = y_s[pl.ds(r0, rt), :].astype(o_ref.dtype)
        return carry

    lax.fori_loop(0, y_s.shape[0] // rt, emit, 0)


def _s5(u2, a_re, a_im, b_re, b_im, c_re, c_im, d, log_dt, seq):
    groups, st = a_re.shape
    grp, step = S5_GROUP, S5_STEP
    m = u2.shape[0]
    g8 = groups // S5_GB
    gw, half = S5_GB * grp, S5_GB * st
    sw = 2 * half
    nc = seq // step
    assert nc & (nc - 1) == 0 and gw == LANES
    r = m // step
    bt = lambda t: jnp.tile(t.transpose(0, 2, 1).reshape(g8, gw, st), (1, 1, 2))
    ct = lambda t: jnp.tile(t.transpose(0, 2, 1).reshape(g8, half, grp), (1, 2, S5_GB))
    rowv = lambda t: jnp.tile(t.reshape(g8, 1, half), (1, 1, 2))
    colv = lambda t: jnp.tile(t.reshape(g8, half, 1), (1, 2, 1))
    ldt = jnp.broadcast_to(log_dt[:, None], (groups, st))
    g3 = lambda shp: pl.BlockSpec((1,) + shp, lambda g: (g, 0, 0))
    return pl.pallas_call(
        functools.partial(_s5_kernel, nc=nc),
        grid=(g8,),
        in_specs=[pl.BlockSpec((m, gw), lambda g: (0, g)),
                  g3((gw, 2 * st)), g3((gw, 2 * st)), g3((sw, gw)), g3((sw, gw)),
                  g3((1, sw)), g3((1, sw)), g3((1, sw)), g3((sw, 1)), g3((sw, 1)), g3((sw, 1)),
                  g3((1, gw))],
        out_specs=pl.BlockSpec((m, gw), lambda g: (0, g)),
        out_shape=jax.ShapeDtypeStruct((m, groups * grp), BF16),
        scratch_shapes=[pltpu.VMEM((r, step * gw), BF16), pltpu.VMEM((step * gw, sw), BF16),
                        pltpu.VMEM((step, sw, gw), BF16), pltpu.VMEM((step * gw, gw), BF16),
                        pltpu.VMEM((m, gw), F32)],
        compiler_params=_params("arbitrary"),
        name="s5_ssm",
    )(u2, bt(b_re), bt(b_im), ct(c_re), ct(c_im), rowv(a_re), rowv(a_im), rowv(ldt),
      colv(a_re), colv(a_im), colv(ldt), d.reshape(g8, 1, gw))


def _sconv_kernel(gb_ref, gc_ref, xc_ref, w_ref, o_ref):
    p = gc_ref[...].astype(F32) * xc_ref[...].astype(F32)
    w = w_ref[...]
    row = lax.broadcasted_iota(jnp.int32, p.shape, 0)
    acc = p * w[C_CONV - 1:C_CONV, :]
    for s in range(1, C_CONV):
        acc = acc + _shift_rows(p, s, row) * w[C_CONV - 1 - s:C_CONV - s, :]
    o_ref[...] = (gb_ref[...].astype(F32) * acc).astype(o_ref.dtype)


def _sconv(proj, conv_w, bsz, seq):
    width = conv_w.shape[1]
    tc = 256
    nb = width // tc
    col = lambda off: pl.BlockSpec((seq, tc), lambda b, j, off=off: (b, off * nb + j))
    return pl.pallas_call(
        _sconv_kernel,
        grid=(bsz, nb),
        in_specs=[col(0), col(1), col(2), pl.BlockSpec((C_CONV, tc), lambda b, j: (0, j))],
        out_specs=pl.BlockSpec((seq, tc), lambda b, j: (b, j)),
        out_shape=jax.ShapeDtypeStruct((bsz * seq, width), BF16),
        compiler_params=_params("arbitrary", "arbitrary"),
        name="short_conv",
    )(proj, proj, proj, conv_w)


NEG_BIG = -1e30


def _dattn_kernel(q_ref, k_ref, vt_ref, lq1_ref, lk1_ref, lq2_ref, lk2_ref, ng_ref, o_ref,
                  acc_s, *, heads, tq, lambda_init):
    dh = DA_HEAD_DIM
    qi = pl.program_id(1)
    scale = dh ** -0.5
    lam = (jnp.exp(jnp.sum(lq1_ref[...] * lk1_ref[...], axis=-1, keepdims=True))
           - jnp.exp(jnp.sum(lq2_ref[...] * lk2_ref[...], axis=-1, keepdims=True)) + lambda_init)
    ki = lax.broadcasted_iota(jnp.int32, (tq, tq), 0)
    qj = lax.broadcasted_iota(jnp.int32, (tq, tq), 1)
    rel = (qj - ki).astype(F32)
    visible = ki <= qj
    ng = ng_ref[...]

    hpi = 2
    for h0 in range(0, heads, hpi):
        slopes = [2.0 ** (-8.0 * (h + 1) / heads) for h in range(h0, h0 + hpi)]

        def block(j, carry, masked, h0=h0, slopes=slopes):
            k0 = pl.multiple_of(j * tq, tq)
            off = ((qi - j) * tq).astype(F32)
            out = []
            for t in range(hpi):
                v_lo = (h0 + t) * 2 * dh
                vt = vt_ref[0, v_lo:v_lo + 2 * dh, pl.ds(k0, tq)]
                bias = -slopes[t] * (rel + off)
                for mp in range(2):
                    c = 2 * t + mp
                    m, l = carry[2 * c], carry[2 * c + 1]
                    lo = v_lo + mp * dh
                    s = _dot_nt(k_ref[pl.ds(k0, tq), lo:lo + dh], q_ref[:, lo:lo + dh]) * scale + bias
                    if masked:
                        s = jnp.where(visible, s, NEG_BIG)
                    m_new = jnp.maximum(m, jnp.max(s, axis=0, keepdims=True))
                    corr = jnp.exp(m - m_new)
                    p = jnp.exp(s - m_new)
                    out += [m_new, corr * l + jnp.sum(p, axis=0, keepdims=True)]
                    acc_s[c] = corr * acc_s[c] + _dot(vt, p.astype(BF16))
            return tuple(out)

        acc_s[...] = jnp.zeros(acc_s.shape, F32)
        neg = jnp.full((1, tq), NEG_BIG, F32)
        zero = jnp.zeros((1, tq), F32)
        carry = lax.fori_loop(0, qi, functools.partial(block, masked=False), (neg, zero) * (2 * hpi))
        carry = block(qi, carry, True)
        for t in range(hpi):
            v_lo = (h0 + t) * 2 * dh
            l1, l2 = carry[4 * t + 1], carry[4 * t + 3]
            o = acc_s[2 * t] / l1 - lam * (acc_s[2 * t + 1] / l2)
            o = o * lax.rsqrt(jnp.mean(o * o, axis=0, keepdims=True) + EPS)
            o_ref[:, v_lo:v_lo + 2 * dh] = (o.T * ng * (1.0 - lambda_init)).astype(o_ref.dtype)


def _dattn(proj, vt, lq1, lk1, lq2, lk2, norm_g, lambda_init, bsz, seq):
    dh = DA_HEAD_DIM
    width = vt.shape[1]
    heads = width // (2 * dh)
    tq = min(256, seq)
    nq = seq // tq
    vec = pl.BlockSpec((1, dh), lambda b, i: (0, 0))
    return pl.pallas_call(
        functools.partial(_dattn_kernel, heads=heads, tq=tq, lambda_init=lambda_init),
        grid=(bsz, nq),
        in_specs=[pl.BlockSpec((tq, width), lambda b, i: (b * nq + i, 3)),
                  pl.BlockSpec((seq, width), lambda b, i: (b, 4)),
                  pl.BlockSpec((1, width, seq), lambda b, i: (b, 0, 0)),
                  vec, vec, vec, vec, pl.BlockSpec((1, 2 * dh), lambda b, i: (0, 0))],
        out_specs=pl.BlockSpec((tq, width), lambda b, i: (b * nq + i, 0)),
        out_shape=jax.ShapeDtypeStruct((bsz * seq, width), BF16),
        scratch_shapes=[pltpu.VMEM((4, 2 * dh, tq), F32)],
        compiler_params=_params("arbitrary", "arbitrary"),
        name="diff_attention",
    )(proj, proj, vt, lq1.reshape(1, dh), lk1.reshape(1, dh), lq2.reshape(1, dh), lk2.reshape(1, dh),
      norm_g.reshape(1, 2 * dh))


def kernel(x, c, ada_w, ada_b, ln1_g, ln1_b, ln2_g, ln2_b, ffn_w_gate, ffn_w_up, ffn_w_down, ab_w_in, ab_w_out, dn_conv_w, dn_a_log, dn_dt_bias, dn_norm_g, s5_a_re, s5_a_im, s5_b_re, s5_b_im, s5_c_re, s5_c_im, s5_d, s5_log_dt, s5_w_glu, cd_w_in, cd_w_out, sc_conv_w, da_lq1, da_lk1, da_lq2, da_lk2, da_norm_g):
    bsz, seq, d = x.shape
    depth = ada_w.shape[0]
    m = bsz * seq
    a_width = dn_conv_w.shape[-1] // 3
    heads = dn_a_log.shape[-1]

    c_pad = jnp.zeros((8, d), F32).at[:bsz].set(c)
    mod = _ada_mod(c_pad, ada_w, ada_b)

    def mvec(i, j):
        return mod[i, :bsz, j * d:(j + 1) * d].reshape(bsz, 1, d)

    x2 = x.reshape(m, d)
    h = _modulate(x2, mvec(0, 1), mvec(0, 0), seq)
    for i in range(depth):
        j = i // 2
        if i % 2 == 0:
            w_u = ab_w_in[j:j + 1, :, 4 * a_width + 2 * heads:]
            w_ba = jnp.pad(ab_w_in[j:j + 1, :, 4 * a_width:4 * a_width + 2 * heads],
                           ((0, 0), (0, 0), (0, LANES - 2 * heads)))
            proj = _mm(h, ab_w_in, j, 4 * a_width, BF16, 1024, 1024, "ab_in_proj")
            uf = _mm(h, w_u, 0, w_u.shape[2], F32, 1024, 1024, "ab_u_proj")
            ba = _mm(h, w_ba, 0, LANES, F32, 1024, LANES, "ab_gate_proj")
            ya = _deltanet(proj, ba, dn_conv_w[j], dn_a_log[j], dn_dt_bias[j], dn_norm_g[j], bsz, seq)
            yb = _s5(uf, s5_a_re[j], s5_a_im[j], s5_b_re[j], s5_b_im[j],
                     s5_c_re[j], s5_c_im[j], s5_d[j], s5_log_dt[j], seq)
            yb = _glu(yb, s5_w_glu[j].astype(BF16), 1024)
            w_out = ab_w_out[j].astype(BF16)
        else:
            lambda_init = 0.8 - 0.6 * math.exp(-0.3 * i)
            c_width = sc_conv_w.shape[-1]
            proj = _mm(h, cd_w_in, j, 5 * c_width, BF16, 1024, 1024, "cd_in_proj")
            vt = _mm_t(h, cd_w_in[j, :, 5 * c_width:].T, bsz, seq, 512, 1024, "cd_value_proj")
            ya = _sconv(proj, sc_conv_w[j], bsz, seq)
            yb = _dattn(proj, vt, da_lq1[j], da_lk1[j], da_lq2[j], da_lk2[j], da_norm_g[j], lambda_init, bsz, seq)
            w_out = cd_w_out[j].astype(BF16)
        x2, h = _res_ln([ya, yb], w_out, x2, mvec(i, 2), ln1_g[i], ln1_b[i],
                        (mvec(i, 4), mvec(i, 3)), seq, 256, "mixer_out_ln")
        act = _gate_up(h, ffn_w_gate, ffn_w_up, i, 1024, 512)
        nxt = (mvec(i + 1, 1), mvec(i + 1, 0)) if i + 1 < depth else None
        x2, h = _res_ln([act], ffn_w_down[i].astype(BF16), x2, mvec(i, 5), ln2_g[i], ln2_b[i],
                        nxt, seq, 256, "ffn_down_ln")
    return x2.reshape(bsz, seq, d)
```

```python
import functools
import math

import jax
import jax.numpy as jnp
from jax import lax
from jax.experimental import pallas as pl
from jax.experimental.pallas import tpu as pltpu

F32 = jnp.float32
BF16 = jnp.bfloat16

DEPTH = 4
A_HEAD_DIM = 128
A_CONV = 4
CHUNK = 64
S5_GROUP = 16
S5_STATE = 64
S5_STEP = 16
C_CONV = 3
DA_HEAD_DIM = 128
ALPHA = (2.0 * DEPTH) ** 0.25
EPS = 1e-5
LANES = 128
VMEM_LIMIT_BYTES = 56 * 1024 * 1024


def _params(*sem):
    return pltpu.CompilerParams(dimension_semantics=sem, vmem_limit_bytes=VMEM_LIMIT_BYTES)


def _silu(x):
    return x * jax.nn.sigmoid(x)


def _dot(a, b):
    return jnp.dot(a, b, preferred_element_type=F32)


def _dot_nt(a, b):
    return lax.dot_general(a, b, (((1,), (1,)), ((), ())), preferred_element_type=F32)


def _dot_tn(a, b):
    return lax.dot_general(a, b, (((0,), (0,)), ((), ())), preferred_element_type=F32)


def _split3(x):
    hi = x.astype(BF16)
    r1 = x - hi.astype(F32)
    mid = r1.astype(BF16)
    lo = (r1 - mid.astype(F32)).astype(BF16)
    return hi, mid, lo


def _dot_f32(a, b):
    a0, a1, a2 = _split3(a)
    b0, b1, b2 = _split3(b)
    return (_dot(a0, b0) + (_dot(a0, b1) + _dot(a1, b0))
            + (_dot(a1, b1) + _dot(a0, b2) + _dot(a2, b0)))


def _ada_kernel(c_ref, w_ref, b_ref, o_ref):
    c = c_ref[...]
    ca = _silu(c).astype(BF16)
    o_ref[0] = _dot(ca, w_ref[0].astype(BF16)) + b_ref[0]


def _ada_mod(c_pad, ada_w, ada_b):
    depth, d, n = ada_w.shape
    rows = c_pad.shape[0]
    tn = 1024
    return pl.pallas_call(
        _ada_kernel,
        grid=(depth, n // tn),
        in_specs=[pl.BlockSpec((rows, d), lambda i, j: (0, 0)),
                  pl.BlockSpec((1, d, tn), lambda i, j: (i, 0, j)),
                  pl.BlockSpec((1, 1, tn), lambda i, j: (i, 0, j))],
        out_specs=pl.BlockSpec((1, rows, tn), lambda i, j: (i, 0, j)),
        out_shape=jax.ShapeDtypeStruct((depth, rows, n), F32),
        compiler_params=_params("arbitrary", "arbitrary"),
        name="ada_mod",
    )(c_pad, ada_w, ada_b.reshape(depth, 1, n))


def _modulate_kernel(x_ref, sc_ref, sh_ref, o_ref):
    o_ref[...] = (x_ref[...] * (1.0 + sc_ref[0]) + sh_ref[0]).astype(o_ref.dtype)


def _modulate(x2, sc, sh, seq):
    m, d = x2.shape
    tl = min(seq, 512)
    nl = seq // tl
    return pl.pallas_call(
        _modulate_kernel,
        grid=(m // tl,),
        in_specs=[pl.BlockSpec((tl, d), lambda i: (i, 0)),
                  pl.BlockSpec((1, 1, d), lambda i: (i // nl, 0, 0)),
                  pl.BlockSpec((1, 1, d), lambda i: (i // nl, 0, 0))],
        out_specs=pl.BlockSpec((tl, d), lambda i: (i, 0)),
        out_shape=jax.ShapeDtypeStruct((m, d), BF16),
        compiler_params=_params("arbitrary"),
        name="modulate",
    )(x2, sc, sh)


def _mm_kernel(a_ref, w_ref, o_ref, w_s):
    @pl.when(pl.program_id(1) == 0)
    def _():
        w_s[...] = w_ref[0].astype(BF16)

    o_ref[...] = _dot(a_ref[...], w_s[...]).astype(o_ref.dtype)


def _mm(a, w, layer, n, out_dtype, tm, tn, name):
    m, k = a.shape
    tm = min(tm, m)
    tn = min(tn, n)
    return pl.pallas_call(
        _mm_kernel,
        grid=(n // tn, m // tm),
        in_specs=[pl.BlockSpec((tm, k), lambda j, i: (i, 0)),
                  pl.BlockSpec((1, k, tn), lambda j, i: (layer, 0, j))],
        out_specs=pl.BlockSpec((tm, tn), lambda j, i: (i, j)),
        out_shape=jax.ShapeDtypeStruct((m, n), out_dtype),
        scratch_shapes=[pltpu.VMEM((k, tn), BF16)],
        compiler_params=_params("arbitrary", "arbitrary"),
        name=name,
    )(a, w)


def _mm_t_kernel(wt_ref, a_ref, o_ref, w_s):
    @pl.when(pl.program_id(1) == 0)
    def _():
        w_s[...] = wt_ref[...].astype(BF16)

    o_ref[0] = _dot_nt(w_s[...], a_ref[...]).astype(o_ref.dtype)


def _mm_t(a, wt, bsz, seq, tn, tl, name):
    m, k = a.shape
    n = wt.shape[0]
    tl = min(tl, seq)
    nl = seq // tl
    return pl.pallas_call(
        _mm_t_kernel,
        grid=(n // tn, m // tl),
        in_specs=[pl.BlockSpec((tn, k), lambda j, i: (j, 0)),
                  pl.BlockSpec((tl, k), lambda j, i: (i, 0))],
        out_specs=pl.BlockSpec((1, tn, tl), lambda j, i: (i // nl, j, i % nl)),
        out_shape=jax.ShapeDtypeStruct((bsz, n, seq), BF16),
        scratch_shapes=[pltpu.VMEM((tn, k), BF16)],
        compiler_params=_params("arbitrary", "arbitrary"),
        name=name,
    )(wt, a)


def _gate_up_kernel(a_ref, wg_ref, wu_ref, wd_ref, o_ref, wdo_ref, wg_s, wu_s):
    @pl.when(pl.program_id(1) == 0)
    def _():
        wg_s[...] = wg_ref[0].astype(BF16)
        wu_s[...] = wu_ref[0].astype(BF16)
        wdo_ref[...] = wd_ref[0].astype(BF16)

    a = a_ref[...]
    g = _dot(a, wg_s[...])
    u = _dot(a, wu_s[...])
    o_ref[...] = (_silu(g) * u).astype(o_ref.dtype)


def _gate_up(a, wg, wu, wd, layer, tm, tn):
    m, k = a.shape
    n = wg.shape[2]
    d = wd.shape[2]
    tm = min(tm, m)
    return pl.pallas_call(
        _gate_up_kernel,
        grid=(n // tn, m // tm),
        in_specs=[pl.BlockSpec((tm, k), lambda j, i: (i, 0)),
                  pl.BlockSpec((1, k, tn), lambda j, i: (layer, 0, j)),
                  pl.BlockSpec((1, k, tn), lambda j, i: (layer, 0, j)),
                  pl.BlockSpec((1, tn, d), lambda j, i: (layer, j, 0))],
        out_specs=[pl.BlockSpec((tm, tn), lambda j, i: (i, j)),
                   pl.BlockSpec((tn, d), lambda j, i: (j, 0))],
        out_shape=[jax.ShapeDtypeStruct((m, n), BF16), jax.ShapeDtypeStruct((n, d), BF16)],
        scratch_shapes=[pltpu.VMEM((k, tn), BF16), pltpu.VMEM((k, tn), BF16)],
        compiler_params=_params("arbitrary", "arbitrary"),
        name="ffn_gate_up",
    )(a, wg, wu, wd)


def _glu_kernel(y_ref, w_ref, o_ref):
    y = y_ref[...]
    t = _dot(y, w_ref[...])
    o_ref[...] = (y.astype(F32) * jax.nn.sigmoid(t)).astype(o_ref.dtype)


def _glu(y, w, tm):
    m, k = y.shape
    tm = min(tm, m)
    return pl.pallas_call(
        _glu_kernel,
        grid=(m // tm,),
        in_specs=[pl.BlockSpec((tm, k), lambda i: (i, 0)),
                  pl.BlockSpec((k, k), lambda i: (0, 0))],
        out_specs=pl.BlockSpec((tm, k), lambda i: (i, 0)),
        out_shape=jax.ShapeDtypeStruct((m, k), BF16),
        compiler_params=_params("arbitrary"),
        name="s5_glu",
    )(y, w)


def _res_ln_kernel(*refs, n_in, has_next):
    a_refs = refs[:n_in]
    w_refs = refs[n_in:2 * n_in]
    x_ref, g_ref, lng_ref, lnb_ref = refs[2 * n_in:2 * n_in + 4]
    pos = 2 * n_in + 4
    if has_next:
        sc_ref, sh_ref = refs[pos:pos + 2]
        pos += 2
    xo_ref = refs[pos]
    pos += 1
    if has_next:
        ho_ref = refs[pos]
        pos += 1
    y_bufs = refs[pos:pos + 2]
    i = pl.program_id(0)

    @pl.when(i == 0)
    def _():
        y_bufs[1][...] = jnp.zeros(y_bufs[1].shape, F32)

    def step(y_new, y_old):
        part = _dot(a_refs[0][...], w_refs[0][...])
        for a_ref, w_ref in zip(a_refs[1:], w_refs[1:]):
            part = part + _dot(a_ref[...], w_ref[...])
        y_new[...] = part
        r = ALPHA * x_ref[...] + (1.0 + g_ref[0]) * y_old[...]
        mu = jnp.mean(r, axis=-1, keepdims=True)
        rc = r - mu
        var = jnp.mean(rc * rc, axis=-1, keepdims=True)
        xn = rc * lax.rsqrt(var + EPS) * lng_ref[...] + lnb_ref[...]
        xo_ref[...] = xn
        if has_next:
            ho_ref[...] = (xn * (1.0 + sc_ref[0]) + sh_ref[0]).astype(ho_ref.dtype)

    even = lax.rem(i, 2) == 0

    @pl.when(even)
    def _():
        step(y_bufs[0], y_bufs[1])

    @pl.when(jnp.logical_not(even))
    def _():
        step(y_bufs[1], y_bufs[0])


def _res_ln(a_list, w, x2, gate, ln_g, ln_b, nxt, seq, tm, name):
    m, d = x2.shape
    n_in = len(a_list)
    ka = a_list[0].shape[1]
    tm = min(tm, m, seq)
    nt = m // tm
    nl = seq // tm
    has_next = nxt is not None
    cur = lambda i: (jnp.minimum(i, nt - 1), 0)
    prev = lambda i: (jnp.maximum(i - 1, 0), 0)
    in_specs = [pl.BlockSpec((tm, ka), cur) for _ in a_list]
    in_specs += [pl.BlockSpec((ka, d), functools.partial(lambda i, s: (s, 0), s=s), pipeline_mode=pl.Buffered(1))
                 for s in range(n_in)]
    bvec = pl.BlockSpec((1, 1, d), lambda i: (jnp.maximum(i - 1, 0) // nl, 0, 0))
    pvec = pl.BlockSpec((1, d), lambda i: (0, 0))
    in_specs += [pl.BlockSpec((tm, d), prev), bvec, pvec, pvec]
    args = list(a_list) + [w] * n_in + [x2, gate, ln_g.reshape(1, d), ln_b.reshape(1, d)]
    out_shape = [jax.ShapeDtypeStruct((m, d), F32)]
    out_specs = [pl.BlockSpec((tm, d), prev)]
    if has_next:
        in_specs += [bvec, bvec]
        args += list(nxt)
        out_shape.append(jax.ShapeDtypeStruct((m, d), BF16))
        out_specs.append(pl.BlockSpec((tm, d), prev))
    out = pl.pallas_call(
        functools.partial(_res_ln_kernel, n_in=n_in, has_next=has_next),
        grid=(nt + 1,),
        in_specs=in_specs,
        out_specs=out_specs,
        out_shape=out_shape,
        scratch_shapes=[pltpu.VMEM((tm, d), F32), pltpu.VMEM((tm, d), F32)],
        compiler_params=_params("arbitrary"),
        name=name,
    )(*args)
    return (out[0], out[1]) if has_next else (out[0], None)


def _shift_rows(x, s, row):
    return jnp.where(row >= s, pltpu.roll(x, s, axis=0), 0.0)


def _conv_silu(x_ref, cw_ref):
    x = x_ref[...].astype(F32)
    w = cw_ref[...]
    row = lax.broadcasted_iota(jnp.int32, x.shape, 0)
    acc = x * w[A_CONV - 1:A_CONV, :]
    for s in range(1, A_CONV):
        acc = acc + _shift_rows(x, s, row) * w[A_CONV - 1 - s:A_CONV - s, :]
    return _silu(acc)


def _l2n(t):
    return t * lax.rsqrt(jnp.sum(t * t, axis=-1, keepdims=True) + 1e-6)


DN_HB = 4


def _delta_kernel(q_ref, k_ref, v_ref, z_ref, cwq_ref, cwk_ref, cwv_ref, a_ref, b_ref,
                  alog_ref, dtb_ref, ng_ref, o_ref,
                  q_s, k_s, v_s, gc_s, bt_s, u_s, wq_s, l2_s, gl_s, s_s, n_s, nt_s, x_s, *, nc, seq):
    c_len, dh = CHUNK, A_HEAD_DIM
    nmat_all = DN_HB * nc
    hh = lax.rem(pl.program_id(1), DN_HB)
    q_s[hh] = _l2n(_conv_silu(q_ref, cwq_ref)) * (dh ** -0.5)
    k_s[hh] = _l2n(_conv_silu(k_ref, cwk_ref))
    v_s[hh] = _conv_silu(v_ref, cwv_ref)

    a = a_ref[0, 0]
    b = b_ref[0, 0]
    sp_in = a + dtb_ref[0]
    softplus = jnp.maximum(sp_in, 0.0) + jnp.log1p(jnp.exp(-jnp.abs(sp_in)))
    g = -jnp.exp(alog_ref[0]) * softplus
    ri = lax.broadcasted_iota(jnp.int32, (c_len, LANES), 0)
    ci = lax.broadcasted_iota(jnp.int32, (c_len, LANES), 1)
    tril, strict, eye = ri >= ci, ri > ci, ri == ci
    eye64 = eye[:, :c_len]
    eye_bf = jnp.where(eye64, 1.0, 0.0).astype(BF16)
    g0, g1, g2 = _split3(g)
    triu = jnp.where(ci < c_len, jnp.where(ri <= ci, 1.0, 0.0), 0.0).astype(BF16)
    gc_s[hh] = _dot(g0, triu) + _dot(g1, triu) + _dot(g2, triu)
    bt_s[hh] = jax.nn.sigmoid(b)

    def prep(c, carry):
        r0 = pl.multiple_of(c * c_len, c_len)
        kc = k_s[hh, pl.ds(r0, c_len), :]
        qc = q_s[hh, pl.ds(r0, c_len), :]
        grow = gc_s[hh, pl.ds(c, 1), :]
        brow = bt_s[hh, pl.ds(c, 1), :]
        gcol = jnp.sum(jnp.where(eye, grow, 0.0), axis=1, keepdims=True)
        bcol = jnp.sum(jnp.where(eye64, brow, 0.0), axis=1, keepdims=True)
        decay = jnp.where(tril, jnp.exp(jnp.where(tril, gcol - grow, 0.0)), 0.0)
        kb = kc.astype(BF16)
        kk = _dot_nt(kb, jnp.concatenate([kb, jnp.zeros_like(kb)], axis=0))
        m0 = pl.multiple_of((hh * nc + c) * c_len, c_len)
        n_s[pl.ds(m0, c_len), :] = jnp.where(strict, kk * bcol * decay, 0.0)
        qk = _dot_nt(qc.astype(BF16), kb)
        glast = grow[:, c_len - 1:c_len]
        kt = (kc * jnp.exp(glast - gcol)).astype(BF16)
        wq_s[hh * nc + c, c_len:, :] = (qc * jnp.exp(gcol)).astype(BF16)
        l2_s[hh * nc + c, :c_len, :] = (qk * decay[:, :c_len]).astype(BF16)
        l2_s[hh * nc + c, c_len:, :] = _dot_tn(kt, eye_bf).astype(BF16)
        gl_s[hh, pl.ds(c, 1), :] = jnp.broadcast_to(jnp.exp(glast), (1, LANES))
        return carry

    lax.fori_loop(0, nc, prep, 0, unroll=2)

    @pl.when(hh == DN_HB - 1)
    def _():
        x_s[...] = jnp.zeros(x_s.shape, F32)
        for i in range(c_len):
            slab = n_s[pl.ds(i, nmat_all, stride=c_len), :]
            nt_s[i] = slab.T[:c_len, :]
        sub = lax.broadcasted_iota(jnp.int32, (8, LANES), 0)
        for ib in range(c_len // 8):
            npc = ib + 1

            def row(ii, carry, ib=ib, npc=npc):
                i = ib * 8 + ii
                parts = [[None] * 4 for _ in range(npc)]
                for j in range(8 * npc):
                    coef = nt_s[i, pl.ds(j, 1), :]
                    for p in range(j // 8 + 1):
                        term = coef * x_s[j, p * 8:(p + 1) * 8, :]
                        k = j % 4
                        parts[p][k] = term if parts[p][k] is None else parts[p][k] + term
                for p in range(npc):
                    live = [t for t in parts[p] if t is not None]
                    tot = live[0]
                    for t in live[1:]:
                        tot = tot + t
                    x_s[i, p * 8:(p + 1) * 8, :] = jnp.where(sub + p * 8 == i, 1.0, 0.0) - tot
                return carry

            lax.fori_loop(0, 8, row, 0)
        zpad = jnp.zeros((LANES - c_len, LANES), F32)
        for i in range(c_len):
            n_s[pl.ds(i, nmat_all, stride=c_len), :] = jnp.concatenate([x_s[i], zpad], axis=0).T

        def uw(hd, c):
            r0 = pl.multiple_of(c * c_len, c_len)
            idx = hd * nc + c
            t = n_s[pl.ds(pl.multiple_of(idx * c_len, c_len), c_len), :][:, :c_len]
            tb = t * bt_s[hd, pl.ds(c, 1), :]
            tw = tb * jnp.exp(gc_s[hd, pl.ds(c, 1), :][:, :c_len])
            kb = k_s[hd, pl.ds(r0, c_len), :].astype(BF16)
            vb = v_s[hd, pl.ds(r0, c_len), :].astype(BF16)
            u_s[hd, pl.ds(r0, c_len), :] = _dot(tb.astype(BF16), vb)
            wq_s[idx, :c_len, :] = _dot(tw.astype(BF16), kb).astype(BF16)

        for hd in range(DN_HB):
            uw(hd, 0)
        ng = ng_ref[...]
        s_s[...] = jnp.zeros(s_s.shape, F32)

        def scan(c, carry):
            r0 = pl.multiple_of(c * c_len, c_len)
            for hd in range(DN_HB):
                s = s_s[hd]
                sb = s.astype(BF16)
                r1 = _dot(wq_s[hd * nc + c], sb)
                v_new = u_s[hd, pl.ds(r0, c_len), :] - r1[:c_len]
                r2 = _dot(l2_s[hd * nc + c], v_new.astype(BF16))
                o = r1[c_len:] + r2[:c_len]
                s_s[hd] = s * gl_s[hd, pl.ds(c, 1), :] + r2[c_len:]
                q_s[hd, pl.ds(r0, c_len), :] = o * lax.rsqrt(jnp.mean(o * o, axis=-1, keepdims=True) + EPS) * ng
            cn = jnp.minimum(c + 1, nc - 1)
            for hd in range(DN_HB):
                uw(hd, cn)
            return carry

        lax.fori_loop(0, nc, scan, 0)

        rt = 256

        def gate(r, carry):
            r0 = pl.multiple_of(r * rt, rt)
            for hd in range(DN_HB):
                zc = z_ref[pl.ds(r0, rt), hd * dh:(hd + 1) * dh].astype(F32)
                o_ref[pl.ds(r0, rt), hd * dh:(hd + 1) * dh] = (q_s[hd, pl.ds(r0, rt), :] * _silu(zc)).astype(o_ref.dtype)
            return carry

        lax.fori_loop(0, seq // rt, gate, 0)


def _deltanet(proj, ba, conv_w, a_log, dt_bias, norm_g, bsz, seq):
    dh = A_HEAD_DIM
    heads = a_log.shape[0]
    nc = seq // CHUNK
    assert DN_HB * nc == LANES and heads % DN_HB == 0
    bh = ba[:, :2 * heads].reshape(bsz, seq, 2, heads).transpose(2, 0, 3, 1).reshape(2, bsz, heads, nc, CHUNK)
    col = lambda off: pl.BlockSpec((seq, dh), lambda b, h, off=off: (b, off + h))
    cw = lambda off: pl.BlockSpec((A_CONV, dh), lambda b, h, off=off: (0, off + h))
    rows = pl.BlockSpec((1, 1, nc, CHUNK), lambda b, h: (b, h, 0, 0))
    scal = pl.BlockSpec((1, 1, 1), lambda b, h: (h, 0, 0))
    hgroups = heads // DN_HB
    return pl.pallas_call(
        functools.partial(_delta_kernel, nc=nc, seq=seq),
        grid=(bsz, heads),
        in_specs=[col(0), col(heads), col(2 * heads),
                  pl.BlockSpec((seq, DN_HB * dh), lambda b, h: (b, 3 * hgroups + h // DN_HB)),
                  cw(0), cw(heads), cw(2 * heads), rows, rows, scal, scal,
                  pl.BlockSpec((1, dh), lambda b, h: (0, 0))],
        out_specs=pl.BlockSpec((seq, DN_HB * dh), lambda b, h: (b, h // DN_HB)),
        out_shape=jax.ShapeDtypeStruct((bsz * seq, heads * dh), BF16),
        scratch_shapes=[pltpu.VMEM((DN_HB, seq, dh), F32), pltpu.VMEM((DN_HB, seq, dh), F32),
                        pltpu.VMEM((DN_HB, seq, dh), F32),
                        pltpu.VMEM((DN_HB, nc, LANES), F32), pltpu.VMEM((DN_HB, nc, CHUNK), F32),
                        pltpu.VMEM((DN_HB, seq, dh), F32),
                        pltpu.VMEM((DN_HB * nc, 2 * CHUNK, dh), BF16), pltpu.VMEM((DN_HB * nc, CHUNK + dh, CHUNK), BF16),
                        pltpu.VMEM((DN_HB, nc, LANES), F32), pltpu.VMEM((DN_HB, dh, dh), F32),
                        pltpu.VMEM((DN_HB * nc * CHUNK, LANES), F32), pltpu.VMEM((CHUNK, CHUNK, LANES), F32),
                        pltpu.VMEM((CHUNK, CHUNK, LANES), F32)],
        compiler_params=_params("arbitrary", "arbitrary"),
        name="gated_deltanet",
    )(proj, proj, proj, proj, conv_w, conv_w, conv_w, bh[1], bh[0],
      a_log.reshape(heads, 1, 1), dt_bias.reshape(heads, 1, 1), norm_g.reshape(1, dh))


def _cexp(re, im):
    m = jnp.exp(re)
    return m * jnp.cos(im), m * jnp.sin(im)


def _cmul(ar, ai, br, bi):
    return ar * br - ai * bi, ar * bi + ai * br


def _gelu_tanh(x):
    return 0.5 * x * (1.0 + jnp.tanh(math.sqrt(2.0 / math.pi) * (x + 0.044715 * (x * x * x))))


S5_GB = LANES // S5_GROUP


def _dot_hi(a, b):
    a0 = a.astype(BF16)
    a1 = (a - a0.astype(F32)).astype(BF16)
    b0 = b.astype(BF16)
    b1 = (b - b0.astype(F32)).astype(BF16)
    return _dot(a0, b0) + (_dot(a0, b1) + _dot(a1, b0))


def _s5_kernel(u_ref, bre_ref, bim_ref, cre_ref, cim_ref, lr_r_ref, li_r_ref, dt_r_ref,
               lr_c_ref, li_c_ref, dt_c_ref, d_ref, o_ref, ua_s, bps_s, cq_s, ks_s, y_s, *, nc):
    step, grp, st = S5_STEP, S5_GROUP, S5_STATE
    gw = S5_GB * grp
    half = S5_GB * st
    sw = 2 * half
    r = ua_s.shape[0]
    gsh, ssh = grp.bit_length() - 1, st.bit_length() - 1

    lr, li = lr_r_ref[0], li_r_ref[0]
    dt = jnp.exp(dt_r_ref[0])
    ar, ai = _cexp(lr * dt, li * dt)
    den = lr * lr + li * li
    pr = ((ar - 1.0) * lr + ai * li) / den
    pi_ = (ai * lr - (ar - 1.0) * li) / den
    re_lane = lax.broadcasted_iota(jnp.int32, (1, sw), 1) < half
    row_g = lax.shift_right_logical(lax.broadcasted_iota(jnp.int32, (gw, sw), 0), gsh)
    lane_g = lax.shift_right_logical(lax.broadcasted_iota(jnp.int32, (gw, sw), 1) & (half - 1), ssh)
    same_g = row_g == lane_g
    br = jnp.where(same_g, jnp.concatenate([bre_ref[0]] * (sw // LANES), axis=1), 0.0)
    bi = jnp.where(same_g, jnp.concatenate([bim_ref[0]] * (sw // LANES), axis=1), 0.0)
    bp0 = None
    for k in range(step):
        blk = br * jnp.where(re_lane, pr, pi_) + bi * jnp.where(re_lane, -pi_, pr)
        if k == 0:
            bp0 = blk
        s = step - 1 - k
        bps_s[s * gw:(s + 1) * gw, :] = blk.astype(BF16)
        pr, pi_ = _cmul(pr, pi_, ar, ai)

    dtc = jnp.exp(dt_c_ref[0])
    acr, aci = _cexp(lr_c_ref[0] * dtc, li_c_ref[0] * dtc)
    acr = jnp.broadcast_to(acr, (sw, gw))
    aci = jnp.broadcast_to(aci, (sw, gw))
    rrow = lax.broadcasted_iota(jnp.int32, (sw, gw), 0)
    re_row = rrow < half
    same_c = (lax.shift_right_logical(rrow & (half - 1), ssh)
              == lax.shift_right_logical(lax.broadcasted_iota(jnp.int32, (sw, gw), 1), gsh))
    pcr = jnp.where(same_c, cre_ref[0], 0.0)
    pci = jnp.where(same_c, cim_ref[0], 0.0)
    for k in range(step + 1):
        cqk = jnp.where(re_row, pcr, -pci)
        if k < step:
            ks_s[(step - 1 - k) * gw:(step - k) * gw, :] = _dot_hi(bp0, cqk).astype(BF16)
            pcr, pci = _cmul(pcr, pci, acr, aci)
        if k >= 1:
            cq_s[k - 1] = cqk.astype(BF16)

    for s in range(step):
        ua_s[:, s * gw:(s + 1) * gw] = u_ref[pl.ds(s, r, stride=step), :].astype(BF16)
    x = _dot(ua_s[...], bps_s[...])
    blk_i = lax.broadcasted_iota(jnp.int32, (r, sw), 0) & (nc - 1)
    er, ei = ar, ai
    for _ in range(step.bit_length() - 1):
        er, ei = _cmul(er, ei, er, ei)
    sh = 1
    while sh < nc:
        xs = jnp.where(blk_i >= sh, pltpu.roll(x, sh, axis=0), 0.0)
        x = x + xs * er + pltpu.roll(xs, half, axis=1) * jnp.where(re_lane, -ei, ei)
        er, ei = _cmul(er, ei, er, ei)
        sh *= 2
    xp = jnp.where(blk_i >= 1, pltpu.roll(x, 1, axis=0), 0.0).astype(BF16)

    d = d_ref[0]
    for t in range(step):
        y = (_dot(ua_s[:, :(t + 1) * gw], ks_s[(step - 1 - t) * gw:, :]) + _dot(xp, cq_s[t])
             + d * u_ref[pl.ds(t, r, stride=step), :])
        y_s[pl.ds(t, r, stride=step), :] = _gelu_tanh(y)

    rt = 512

    def emit(i, carry):
        r0 = pl.multiple_of(i * rt, rt)
        o_ref[pl.ds(r0, rt), :] = y_s[pl.ds(r0, rt), :].astype(o_ref.dtype)
        return carry

    lax.fori_loop(0, y_s.shape[0] // rt, emit, 0)


def _s5(u2, a_re, a_im, b_re, b_im, c_re, c_im, d, log_dt, seq):
    groups, st = a_re.shape
    grp, step = S5_GROUP, S5_STEP
    m = u2.shape[0]
    g8 = groups // S5_GB
    gw, half = S5_GB * grp, S5_GB * st
    sw = 2 * half
    nc = seq // step
    assert nc & (nc - 1) == 0 and gw == LANES
    r = m // step
    bt = lambda t: jnp.tile(t.transpose(0, 2, 1).reshape(g8, gw, st), (1, 1, 2))
    ct = lambda t: jnp.tile(t.transpose(0, 2, 1).reshape(g8, half, grp), (1, 2, S5_GB))
    rowv = lambda t: jnp.tile(t.reshape(g8, 1, half), (1, 1, 2))
    colv = lambda t: jnp.tile(t.reshape(g8, half, 1), (1, 2, 1))
    ldt = jnp.broadcast_to(log_dt[:, None], (groups, st))
    g3 = lambda shp: pl.BlockSpec((1,) + shp, lambda g: (g, 0, 0))
    return pl.pallas_call(
        functools.partial(_s5_kernel, nc=nc),
        grid=(g8,),
        in_specs=[pl.BlockSpec((m, gw), lambda g: (0, g)),
                  g3((gw, 2 * st)), g3((gw, 2 * st)), g3((sw, gw)), g3((sw, gw)),
                  g3((1, sw)), g3((1, sw)), g3((1, sw)), g3((sw, 1)), g3((sw, 1)), g3((sw, 1)),
                  g3((1, gw))],
        out_specs=pl.BlockSpec((m, gw), lambda g: (0, g)),
        out_shape=jax.ShapeDtypeStruct((m, groups * grp), BF16),
        scratch_shapes=[pltpu.VMEM((r, step * gw), BF16), pltpu.VMEM((step * gw, sw), BF16),
                        pltpu.VMEM((step, sw, gw), BF16), pltpu.VMEM((step * gw, gw), BF16),
                        pltpu.VMEM((m, gw), F32)],
        compiler_params=_params("arbitrary"),
        name="s5_ssm",
    )(u2, bt(b_re), bt(b_im), ct(c_re), ct(c_im), rowv(a_re), rowv(a_im), rowv(ldt),
      colv(a_re), colv(a_im), colv(ldt), d.reshape(g8, 1, gw))


def _sconv_kernel(gb_ref, gc_ref, xc_ref, w_ref, o_ref):
    p = gc_ref[...].astype(F32) * xc_ref[...].astype(F32)
    w = w_ref[...]
    row = lax.broadcasted_iota(jnp.int32, p.shape, 0)
    acc = p * w[C_CONV - 1:C_CONV, :]
    for s in range(1, C_CONV):
        acc = acc + _shift_rows(p, s, row) * w[C_CONV - 1 - s:C_CONV - s, :]
    o_ref[...] = (gb_ref[...].astype(F32) * acc).astype(o_ref.dtype)


def _sconv(proj, conv_w, bsz, seq):
    width = conv_w.shape[1]
    tc = 256
    nb = width // tc
    col = lambda off: pl.BlockSpec((seq, tc), lambda b, j, off=off: (b, off * nb + j))
    return pl.pallas_call(
        _sconv_kernel,
        grid=(bsz, nb),
        in_specs=[col(0), col(1), col(2), pl.BlockSpec((C_CONV, tc), lambda b, j: (0, j))],
        out_specs=pl.BlockSpec((seq, tc), lambda b, j: (b, j)),
        out_shape=jax.ShapeDtypeStruct((bsz * seq, width), BF16),
        compiler_params=_params("arbitrary", "arbitrary"),
        name="short_conv",
    )(proj, proj, proj, conv_w)


NEG_BIG = -1e30


def _dattn_kernel(q_ref, k_ref, vt_ref, lq1_ref, lk1_ref, lq2_ref, lk2_ref, ng_ref, o_ref,
                  acc_s, *, heads, tq, lambda_init):
    dh = DA_HEAD_DIM
    qi = pl.program_id(1)
    scale = dh ** -0.5
    lam = (jnp.exp(jnp.sum(lq1_ref[...] * lk1_ref[...], axis=-1, keepdims=True))
           - jnp.exp(jnp.sum(lq2_ref[...] * lk2_ref[...], axis=-1, keepdims=True)) + lambda_init)
    ki = lax.broadcasted_iota(jnp.int32, (tq, tq), 0)
    qj = lax.broadcasted_iota(jnp.int32, (tq, tq), 1)
    rel = (qj - ki).astype(F32)
    visible = ki <= qj
    ng = ng_ref[...]

    hpi = 2
    for h0 in range(0, heads, hpi):
        slopes = [2.0 ** (-8.0 * (h + 1) / heads) for h in range(h0, h0 + hpi)]

        def block(j, carry, masked, h0=h0, slopes=slopes):
            k0 = pl.multiple_of(j * tq, tq)
            off = ((qi - j) * tq).astype(F32)
            out = []
            for t in range(hpi):
                v_lo = (h0 + t) * 2 * dh
                vt = vt_ref[0, v_lo:v_lo + 2 * dh, pl.ds(k0, tq)]
                bias = -slopes[t] * (rel + off)
                for mp in range(2):
                    c = 2 * t + mp
                    m, l = carry[2 * c], carry[2 * c + 1]
                    lo = v_lo + mp * dh
                    s = _dot_nt(k_ref[pl.ds(k0, tq), lo:lo + dh], q_ref[:, lo:lo + dh]) * scale + bias
                    if masked:
                        s = jnp.where(visible, s, NEG_BIG)
                    m_new = jnp.maximum(m, jnp.max(s, axis=0, keepdims=True))
                    corr = jnp.exp(m - m_new)
                    p = jnp.exp(s - m_new)
                    out += [m_new, corr * l + jnp.sum(p, axis=0, keepdims=True)]
                    acc_s[c] = corr * acc_s[c] + _dot(vt, p.astype(BF16))
            return tuple(out)

        acc_s[...] = jnp.zeros(acc_s.shape, F32)
        neg = jnp.full((1, tq), NEG_BIG, F32)
        zero = jnp.zeros((1, tq), F32)
        carry = lax.fori_loop(0, qi, functools.partial(block, masked=False), (neg, zero) * (2 * hpi))
        carry = block(qi, carry, True)
        for t in range(hpi):
            v_lo = (h0 + t) * 2 * dh
            l1, l2 = carry[4 * t + 1], carry[4 * t + 3]
            o = acc_s[2 * t] / l1 - lam * (acc_s[2 * t + 1] / l2)
            o = o * lax.rsqrt(jnp.mean(o * o, axis=0, keepdims=True) + EPS)
            o_ref[:, v_lo:v_lo + 2 * dh] = (o.T * ng * (1.0 - lambda_init)).astype(o_ref.dtype)


def _dattn(proj, vt, lq1, lk1, lq2, lk2, norm_g, lambda_init, bsz, seq):
    dh = DA_HEAD_DIM
    width = vt.shape[1]
    heads = width // (2 * dh)
    tq = min(256, seq)
    nq = seq // tq
    vec = pl.BlockSpec((1, dh), lambda b, i: (0, 0))
    return pl.pallas_call(
        functools.partial(_dattn_kernel, heads=heads, tq=tq, lambda_init=lambda_init),
        grid=(bsz, nq),
        in_specs=[pl.BlockSpec((tq, width), lambda b, i: (b * nq + i, 3)),
                  pl.BlockSpec((seq, width), lambda b, i: (b, 4)),
                  pl.BlockSpec((1, width, seq), lambda b, i: (b, 0, 0)),
                  vec, vec, vec, vec, pl.BlockSpec((1, 2 * dh), lambda b, i: (0, 0))],
        out_specs=pl.BlockSpec((tq, width), lambda b, i: (b * nq + i, 0)),
        out_shape=jax.ShapeDtypeStruct((bsz * seq, width), BF16),
        scratch_shapes=[pltpu.VMEM((4, 2 * dh, tq), F32)],
        compiler_params=_params("arbitrary", "arbitrary"),
        name="diff_attention",
    )(proj, proj, vt, lq1.reshape(1, dh), lk1.reshape(1, dh), lq2.reshape(1, dh), lk2.reshape(1, dh),
      norm_g.reshape(1, 2 * dh))


def kernel(x, c, ada_w, ada_b, ln1_g, ln1_b, ln2_g, ln2_b, ffn_w_gate, ffn_w_up, ffn_w_down, ab_w_in, ab_w_out, dn_conv_w, dn_a_log, dn_dt_bias, dn_norm_g, s5_a_re, s5_a_im, s5_b_re, s5_b_im, s5_c_re, s5_c_im, s5_d, s5_log_dt, s5_w_glu, cd_w_in, cd_w_out, sc_conv_w, da_lq1, da_lk1, da_lq2, da_lk2, da_norm_g):
    bsz, seq, d = x.shape
    depth = ada_w.shape[0]
    m = bsz * seq
    a_width = dn_conv_w.shape[-1] // 3
    heads = dn_a_log.shape[-1]

    c_pad = jnp.zeros((8, d), F32).at[:bsz].set(c)
    mod = _ada_mod(c_pad, ada_w, ada_b)

    def mvec(i, j):
        return mod[i, :bsz, j * d:(j + 1) * d].reshape(bsz, 1, d)

    x2 = x.reshape(m, d)
    h = _modulate(x2, mvec(0, 1), mvec(0, 0), seq)
    for i in range(depth):
        j = i // 2
        if i % 2 == 0:
            w_uba = jnp.pad(jnp.concatenate([ab_w_in[j:j + 1, :, 4 * a_width + 2 * heads:],
                                             ab_w_in[j:j + 1, :, 4 * a_width:4 * a_width + 2 * heads]], axis=2),
                            ((0, 0), (0, 0), (0, LANES - 2 * heads)))
            b_width = w_uba.shape[2] - LANES
            proj = _mm(h, ab_w_in, j, 4 * a_width, BF16, 1024, 1024, "ab_in_proj")
            uf = _mm(h, w_uba, 0, w_uba.shape[2], F32, 1024, w_uba.shape[2], "ab_u_proj")
            ya = _deltanet(proj, uf[:, b_width:], dn_conv_w[j], dn_a_log[j], dn_dt_bias[j], dn_norm_g[j], bsz, seq)
            yb = _s5(uf, s5_a_re[j], s5_a_im[j], s5_b_re[j], s5_b_im[j],
                     s5_c_re[j], s5_c_im[j], s5_d[j], s5_log_dt[j], seq)
            yb = _glu(yb, s5_w_glu[j].astype(BF16), 1024)
            w_out = ab_w_out[j].astype(BF16)
        else:
            lambda_init = 0.8 - 0.6 * math.exp(-0.3 * i)
            c_width = sc_conv_w.shape[-1]
            proj = _mm(h, cd_w_in, j, 5 * c_width, BF16, 1024, 1024, "cd_in_proj")
            vt = _mm_t(h, cd_w_in[j, :, 5 * c_width:].T, bsz, seq, 512, 1024, "cd_value_proj")
            ya = _sconv(proj, sc_conv_w[j], bsz, seq)
            yb = _dattn(proj, vt, da_lq1[j], da_lk1[j], da_lq2[j], da_lk2[j], da_norm_g[j], lambda_init, bsz, seq)
            w_out = cd_w_out[j].astype(BF16)
        x2, h = _res_ln([ya, yb], w_out, x2, mvec(i, 2), ln1_g[i], ln1_b[i],
                        (mvec(i, 4), mvec(i, 3)), seq, 512, "mixer_out_ln")
        act, w_down = _gate_up(h, ffn_w_gate, ffn_w_up, ffn_w_down, i, 1024, 512)
        nxt = (mvec(i + 1, 1), mvec(i + 1, 0)) if i + 1 < depth else None
        x2, h = _res_ln([act], w_down, x2, mvec(i, 5), ln2_g[i], ln2_b[i],
                        nxt, seq, 256, "ffn_down_ln")
    return x2.reshape(bsz, seq, d)
```

```python
import functools
import math

import jax
import jax.numpy as jnp
from jax import lax
from jax.experimental import pallas as pl
from jax.experimental.pallas import tpu as pltpu

F32 = jnp.float32
BF16 = jnp.bfloat16

DEPTH = 4
A_HEAD_DIM = 128
A_CONV = 4
CHUNK = 64
S5_GROUP = 16
S5_STATE = 64
S5_STEP = 16
C_CONV = 3
DA_HEAD_DIM = 128
ALPHA = (2.0 * DEPTH) ** 0.25
EPS = 1e-5
LANES = 128
VMEM_LIMIT_BYTES = 56 * 1024 * 1024


def _params(*sem):
    return pltpu.CompilerParams(dimension_semantics=sem, vmem_limit_bytes=VMEM_LIMIT_BYTES)


def _silu(x):
    return x * jax.nn.sigmoid(x)


def _dot(a, b):
    return jnp.dot(a, b, preferred_element_type=F32)


def _dot_nt(a, b):
    return lax.dot_general(a, b, (((1,), (1,)), ((), ())), preferred_element_type=F32)


def _dot_tn(a, b):
    return lax.dot_general(a, b, (((0,), (0,)), ((), ())), preferred_element_type=F32)


def _split3(x):
    hi = x.astype(BF16)
    r1 = x - hi.astype(F32)
    mid = r1.astype(BF16)
    lo = (r1 - mid.astype(F32)).astype(BF16)
    return hi, mid, lo


def _dot_f32(a, b):
    a0, a1, a2 = _split3(a)
    b0, b1, b2 = _split3(b)
    return (_dot(a0, b0) + (_dot(a0, b1) + _dot(a1, b0))
            + (_dot(a1, b1) + _dot(a0, b2) + _dot(a2, b0)))


def _ada_kernel(c_ref, w_ref, b_ref, o_ref):
    c = c_ref[...]
    ca = _silu(c).astype(BF16)
    o_ref[0] = _dot(ca, w_ref[0].astype(BF16)) + b_ref[0]


def _ada_mod(c_pad, ada_w, ada_b):
    depth, d, n = ada_w.shape
    rows = c_pad.shape[0]
    tn = 1024
    return pl.pallas_call(
        _ada_kernel,
        grid=(depth, n // tn),
        in_specs=[pl.BlockSpec((rows, d), lambda i, j: (0, 0)),
                  pl.BlockSpec((1, d, tn), lambda i, j: (i, 0, j)),
                  pl.BlockSpec((1, 1, tn), lambda i, j: (i, 0, j))],
        out_specs=pl.BlockSpec((1, rows, tn), lambda i, j: (i, 0, j)),
        out_shape=jax.ShapeDtypeStruct((depth, rows, n), F32),
        compiler_params=_params("arbitrary", "arbitrary"),
        name="ada_mod",
    )(c_pad, ada_w, ada_b.reshape(depth, 1, n))


def _modulate_kernel(x_ref, sc_ref, sh_ref, o_ref):
    o_ref[...] = (x_ref[...] * (1.0 + sc_ref[0]) + sh_ref[0]).astype(o_ref.dtype)


def _modulate(x2, sc, sh, seq):
    m, d = x2.shape
    tl = min(seq, 512)
    nl = seq // tl
    return pl.pallas_call(
        _modulate_kernel,
        grid=(m // tl,),
        in_specs=[pl.BlockSpec((tl, d), lambda i: (i, 0)),
                  pl.BlockSpec((1, 1, d), lambda i: (i // nl, 0, 0)),
                  pl.BlockSpec((1, 1, d), lambda i: (i // nl, 0, 0))],
        out_specs=pl.BlockSpec((tl, d), lambda i: (i, 0)),
        out_shape=jax.ShapeDtypeStruct((m, d), BF16),
        compiler_params=_params("arbitrary"),
        name="modulate",
    )(x2, sc, sh)


def _mm_kernel(a_ref, w_ref, o_ref, w_s):
    @pl.when(pl.program_id(1) == 0)
    def _():
        w_s[...] = w_ref[0].astype(BF16)

    o_ref[...] = _dot(a_ref[...], w_s[...]).astype(o_ref.dtype)


def _mm(a, w, layer, n, out_dtype, tm, tn, name):
    m, k = a.shape
    tm = min(tm, m)
    tn = min(tn, n)
    return pl.pallas_call(
        _mm_kernel,
        grid=(n // tn, m // tm),
        in_specs=[pl.BlockSpec((tm, k), lambda j, i: (i, 0)),
                  pl.BlockSpec((1, k, tn), lambda j, i: (layer, 0, j))],
        out_specs=pl.BlockSpec((tm, tn), lambda j, i: (i, j)),
        out_shape=jax.ShapeDtypeStruct((m, n), out_dtype),
        scratch_shapes=[pltpu.VMEM((k, tn), BF16)],
        compiler_params=_params("arbitrary", "arbitrary"),
        name=name,
    )(a, w)


def _mm_t_kernel(wt_ref, a_ref, o_ref, w_s):
    @pl.when(pl.program_id(1) == 0)
    def _():
        w_s[...] = wt_ref[...].astype(BF16)

    o_ref[0] = _dot_nt(w_s[...], a_ref[...]).astype(o_ref.dtype)


def _mm_t(a, wt, bsz, seq, tn, tl, name):
    m, k = a.shape
    n = wt.shape[0]
    tl = min(tl, seq)
    nl = seq // tl
    return pl.pallas_call(
        _mm_t_kernel,
        grid=(n // tn, m // tl),
        in_specs=[pl.BlockSpec((tn, k), lambda j, i: (j, 0)),
                  pl.BlockSpec((tl, k), lambda j, i: (i, 0))],
        out_specs=pl.BlockSpec((1, tn, tl), lambda j, i: (i // nl, j, i % nl)),
        out_shape=jax.ShapeDtypeStruct((bsz, n, seq), BF16),
        scratch_shapes=[pltpu.VMEM((tn, k), BF16)],
        compiler_params=_params("arbitrary", "arbitrary"),
        name=name,
    )(wt, a)


def _gate_up_kernel(a_ref, wg_ref, wu_ref, o_ref, wg_s, wu_s):
    @pl.when(pl.program_id(1) == 0)
    def _():
        wg_s[...] = wg_ref[0].astype(BF16)
        wu_s[...] = wu_ref[0].astype(BF16)

    a = a_ref[...]
    g = _dot(a, wg_s[...])
    u = _dot(a, wu_s[...])
    o_ref[...] = (_silu(g) * u).astype(o_ref.dtype)


def _gate_up(a, wg, wu, layer, tm, tn):
    m, k = a.shape
    n = wg.shape[2]
    tm = min(tm, m)
    return pl.pallas_call(
        _gate_up_kernel,
        grid=(n // tn, m // tm),
        in_specs=[pl.BlockSpec((tm, k), lambda j, i: (i, 0)),
                  pl.BlockSpec((1, k, tn), lambda j, i: (layer, 0, j)),
                  pl.BlockSpec((1, k, tn), lambda j, i: (layer, 0, j))],
        out_specs=pl.BlockSpec((tm, tn), lambda j, i: (i, j)),
        out_shape=jax.ShapeDtypeStruct((m, n), BF16),
        scratch_shapes=[pltpu.VMEM((k, tn), BF16), pltpu.VMEM((k, tn), BF16)],
        compiler_params=_params("arbitrary", "arbitrary"),
        name="ffn_gate_up",
    )(a, wg, wu)


def _glu_kernel(y_ref, w_ref, o_ref):
    y = y_ref[...]
    t = _dot(y, w_ref[...])
    o_ref[...] = (y.astype(F32) * jax.nn.sigmoid(t)).astype(o_ref.dtype)


def _glu(y, w, tm):
    m, k = y.shape
    tm = min(tm, m)
    return pl.pallas_call(
        _glu_kernel,
        grid=(m // tm,),
        in_specs=[pl.BlockSpec((tm, k), lambda i: (i, 0)),
                  pl.BlockSpec((k, k), lambda i: (0, 0))],
        out_specs=pl.BlockSpec((tm, k), lambda i: (i, 0)),
        out_shape=jax.ShapeDtypeStruct((m, k), BF16),
        compiler_params=_params("arbitrary"),
        name="s5_glu",
    )(y, w)


def _res_ln_kernel(*refs, n_in, has_next):
    a_refs = refs[:n_in]
    w_refs = refs[n_in:2 * n_in]
    x_ref, g_ref, lng_ref, lnb_ref = refs[2 * n_in:2 * n_in + 4]
    pos = 2 * n_in + 4
    if has_next:
        sc_ref, sh_ref = refs[pos:pos + 2]
        pos += 2
    xo_ref = refs[pos]
    pos += 1
    if has_next:
        ho_ref = refs[pos]
        pos += 1
    y_bufs = refs[pos:pos + 2]
    i = pl.program_id(0)

    @pl.when(i == 0)
    def _():
        y_bufs[1][...] = jnp.zeros(y_bufs[1].shape, F32)

    def step(y_new, y_old):
        part = _dot(a_refs[0][...], w_refs[0][...])
        for a_ref, w_ref in zip(a_refs[1:], w_refs[1:]):
            part = part + _dot(a_ref[...], w_ref[...])
        y_new[...] = part
        r = ALPHA * x_ref[...] + (1.0 + g_ref[0]) * y_old[...]
        mu = jnp.mean(r, axis=-1, keepdims=True)
        rc = r - mu
        var = jnp.mean(rc * rc, axis=-1, keepdims=True)
        xn = rc * lax.rsqrt(var + EPS) * lng_ref[...] + lnb_ref[...]
        xo_ref[...] = xn
        if has_next:
            ho_ref[...] = (xn * (1.0 + sc_ref[0]) + sh_ref[0]).astype(ho_ref.dtype)

    even = lax.rem(i, 2) == 0

    @pl.when(even)
    def _():
        step(y_bufs[0], y_bufs[1])

    @pl.when(jnp.logical_not(even))
    def _():
        step(y_bufs[1], y_bufs[0])


def _res_ln(a_list, w, x2, gate, ln_g, ln_b, nxt, seq, tm, name):
    m, d = x2.shape
    n_in = len(a_list)
    ka = a_list[0].shape[1]
    tm = min(tm, m, seq)
    nt = m // tm
    nl = seq // tm
    has_next = nxt is not None
    cur = lambda i: (jnp.minimum(i, nt - 1), 0)
    prev = lambda i: (jnp.maximum(i - 1, 0), 0)
    in_specs = [pl.BlockSpec((tm, ka), cur) for _ in a_list]
    in_specs += [pl.BlockSpec((ka, d), functools.partial(lambda i, s: (s, 0), s=s), pipeline_mode=pl.Buffered(1))
                 for s in range(n_in)]
    bvec = pl.BlockSpec((1, 1, d), lambda i: (jnp.maximum(i - 1, 0) // nl, 0, 0))
    pvec = pl.BlockSpec((1, d), lambda i: (0, 0))
    in_specs += [pl.BlockSpec((tm, d), prev), bvec, pvec, pvec]
    args = list(a_list) + [w] * n_in + [x2, gate, ln_g.reshape(1, d), ln_b.reshape(1, d)]
    out_shape = [jax.ShapeDtypeStruct((m, d), F32)]
    out_specs = [pl.BlockSpec((tm, d), prev)]
    if has_next:
        in_specs += [bvec, bvec]
        args += list(nxt)
        out_shape.append(jax.ShapeDtypeStruct((m, d), BF16))
        out_specs.append(pl.BlockSpec((tm, d), prev))
    out = pl.pallas_call(
        functools.partial(_res_ln_kernel, n_in=n_in, has_next=has_next),
        grid=(nt + 1,),
        in_specs=in_specs,
        out_specs=out_specs,
        out_shape=out_shape,
        scratch_shapes=[pltpu.VMEM((tm, d), F32), pltpu.VMEM((tm, d), F32)],
        compiler_params=_params("arbitrary"),
        name=name,
    )(*args)
    return (out[0], out[1]) if has_next else (out[0], None)


def _shift_rows(x, s, row):
    return jnp.where(row >= s, pltpu.roll(x, s, axis=0), 0.0)


def _conv_silu(x_ref, cw_ref):
    x = x_ref[...].astype(F32)
    w = cw_ref[...]
    row = lax.broadcasted_iota(jnp.int32, x.shape, 0)
    acc = x * w[A_CONV - 1:A_CONV, :]
    for s in range(1, A_CONV):
        acc = acc + _shift_rows(x, s, row) * w[A_CONV - 1 - s:A_CONV - s, :]
    return _silu(acc)


def _l2n(t):
    return t * lax.rsqrt(jnp.sum(t * t, axis=-1, keepdims=True) + 1e-6)


DN_HB = 4


def _delta_kernel(q_ref, k_ref, v_ref, z_ref, cwq_ref, cwk_ref, cwv_ref, a_ref, b_ref,
                  alog_ref, dtb_ref, ng_ref, o_ref,
                  q_s, k_s, v_s, gc_s, bt_s, u_s, wq_s, l2_s, gl_s, s_s, n_s, nt_s, x_s, *, nc, seq):
    c_len, dh = CHUNK, A_HEAD_DIM
    nmat_all = DN_HB * nc
    hh = lax.rem(pl.program_id(1), DN_HB)
    q_s[hh] = _l2n(_conv_silu(q_ref, cwq_ref)) * (dh ** -0.5)
    k_s[hh] = _l2n(_conv_silu(k_ref, cwk_ref))
    v_s[hh] = _conv_silu(v_ref, cwv_ref)

    a = a_ref[0, 0]
    b = b_ref[0, 0]
    sp_in = a + dtb_ref[0]
    softplus = jnp.maximum(sp_in, 0.0) + jnp.log1p(jnp.exp(-jnp.abs(sp_in)))
    g = -jnp.exp(alog_ref[0]) * softplus
    ri = lax.broadcasted_iota(jnp.int32, (c_len, LANES), 0)
    ci = lax.broadcasted_iota(jnp.int32, (c_len, LANES), 1)
    tril, strict, eye = ri >= ci, ri > ci, ri == ci
    eye64 = eye[:, :c_len]
    eye_bf = jnp.where(eye64, 1.0, 0.0).astype(BF16)
    g0, g1, g2 = _split3(g)
    triu = jnp.where(ci < c_len, jnp.where(ri <= ci, 1.0, 0.0), 0.0).astype(BF16)
    gc_s[hh] = _dot(g0, triu) + _dot(g1, triu) + _dot(g2, triu)
    bt_s[hh] = jax.nn.sigmoid(b)

    def prep(c, carry):
        r0 = pl.multiple_of(c * c_len, c_len)
        for hd in range(DN_HB):
            kc = k_s[hd, pl.ds(r0, c_len), :]
            qc = q_s[hd, pl.ds(r0, c_len), :]
            grow = gc_s[hd, pl.ds(c, 1), :]
            brow = bt_s[hd, pl.ds(c, 1), :]
            gcol = jnp.sum(jnp.where(eye, grow, 0.0), axis=1, keepdims=True)
            bcol = jnp.sum(jnp.where(eye64, brow, 0.0), axis=1, keepdims=True)
            decay = jnp.where(tril, jnp.exp(jnp.where(tril, gcol - grow, 0.0)), 0.0)
            kb = kc.astype(BF16)
            kk = _dot_nt(kb, jnp.concatenate([kb, jnp.zeros_like(kb)], axis=0))
            m0 = pl.multiple_of((hd * nc + c) * c_len, c_len)
            n_s[pl.ds(m0, c_len), :] = jnp.where(strict, kk * bcol * decay, 0.0)
            qk = _dot_nt(qc.astype(BF16), kb)
            glast = grow[:, c_len - 1:c_len]
            kt = (kc * jnp.exp(glast - gcol)).astype(BF16)
            wq_s[hd * nc + c, c_len:, :] = (qc * jnp.exp(gcol)).astype(BF16)
            l2_s[hd * nc + c, :c_len, :] = (qk * decay[:, :c_len]).astype(BF16)
            l2_s[hd * nc + c, c_len:, :] = _dot_tn(kt, eye_bf).astype(BF16)
            gl_s[hd, pl.ds(c, 1), :] = jnp.broadcast_to(jnp.exp(glast), (1, LANES))
        return carry

    @pl.when(hh == DN_HB - 1)
    def _():
        lax.fori_loop(0, nc, prep, 0)
        x_s[...] = jnp.zeros(x_s.shape, F32)
        for i in range(c_len):
            slab = n_s[pl.ds(i, nmat_all, stride=c_len), :]
            nt_s[i] = slab.T[:c_len, :]
        sub = lax.broadcasted_iota(jnp.int32, (8, LANES), 0)
        for ib in range(c_len // 8):
            npc = ib + 1

            def row(ii, carry, ib=ib, npc=npc):
                i = ib * 8 + ii
                parts = [[None] * 4 for _ in range(npc)]
                for j in range(8 * npc):
                    coef = nt_s[i, pl.ds(j, 1), :]
                    for p in range(j // 8 + 1):
                        term = coef * x_s[j, p * 8:(p + 1) * 8, :]
                        k = j % 4
                        parts[p][k] = term if parts[p][k] is None else parts[p][k] + term
                for p in range(npc):
                    live = [t for t in parts[p] if t is not None]
                    tot = live[0]
                    for t in live[1:]:
                        tot = tot + t
                    x_s[i, p * 8:(p + 1) * 8, :] = jnp.where(sub + p * 8 == i, 1.0, 0.0) - tot
                return carry

            lax.fori_loop(0, 8, row, 0)
        zpad = jnp.zeros((LANES - c_len, LANES), F32)
        for i in range(c_len):
            n_s[pl.ds(i, nmat_all, stride=c_len), :] = jnp.concatenate([x_s[i], zpad], axis=0).T

        def uw(hd, c):
            r0 = pl.multiple_of(c * c_len, c_len)
            idx = hd * nc + c
            t = n_s[pl.ds(pl.multiple_of(idx * c_len, c_len), c_len), :][:, :c_len]
            tb = t * bt_s[hd, pl.ds(c, 1), :]
            tw = tb * jnp.exp(gc_s[hd, pl.ds(c, 1), :][:, :c_len])
            kb = k_s[hd, pl.ds(r0, c_len), :].astype(BF16)
            vb = v_s[hd, pl.ds(r0, c_len), :].astype(BF16)
            u_s[hd, pl.ds(r0, c_len), :] = _dot(tb.astype(BF16), vb)
            wq_s[idx, :c_len, :] = _dot(tw.astype(BF16), kb).astype(BF16)

        for hd in range(DN_HB):
            uw(hd, 0)
        ng = ng_ref[...]
        s_s[...] = jnp.zeros(s_s.shape, F32)

        def scan(c, carry):
            r0 = pl.multiple_of(c * c_len, c_len)
            for hd in range(DN_HB):
                s = s_s[hd]
                sb = s.astype(BF16)
                r1 = _dot(wq_s[hd * nc + c], sb)
                v_new = u_s[hd, pl.ds(r0, c_len), :] - r1[:c_len]
                r2 = _dot(l2_s[hd * nc + c], v_new.astype(BF16))
                o = r1[c_len:] + r2[:c_len]
                s_s[hd] = s * gl_s[hd, pl.ds(c, 1), :] + r2[c_len:]
                q_s[hd, pl.ds(r0, c_len), :] = o * lax.rsqrt(jnp.mean(o * o, axis=-1, keepdims=True) + EPS) * ng
            cn = jnp.minimum(c + 1, nc - 1)
            for hd in range(DN_HB):
                uw(hd, cn)
            return carry

        lax.fori_loop(0, nc, scan, 0)

        rt = 256

        def gate(r, carry):
            r0 = pl.multiple_of(r * rt, rt)
            for hd in range(DN_HB):
                zc = z_ref[pl.ds(r0, rt), hd * dh:(hd + 1) * dh].astype(F32)
                o_ref[pl.ds(r0, rt), hd * dh:(hd + 1) * dh] = (q_s[hd, pl.ds(r0, rt), :] * _silu(zc)).astype(o_ref.dtype)
            return carry

        lax.fori_loop(0, seq // rt, gate, 0)


def _deltanet(proj, ba, conv_w, a_log, dt_bias, norm_g, bsz, seq):
    dh = A_HEAD_DIM
    heads = a_log.shape[0]
    nc = seq // CHUNK
    assert DN_HB * nc == LANES and heads % DN_HB == 0
    bh = ba[:, :2 * heads].reshape(bsz, seq, 2, heads).transpose(2, 0, 3, 1).reshape(2, bsz, heads, nc, CHUNK)
    col = lambda off: pl.BlockSpec((seq, dh), lambda b, h, off=off: (b, off + h))
    cw = lambda off: pl.BlockSpec((A_CONV, dh), lambda b, h, off=off: (0, off + h))
    rows = pl.BlockSpec((1, 1, nc, CHUNK), lambda b, h: (b, h, 0, 0))
    scal = pl.BlockSpec((1, 1, 1), lambda b, h: (h, 0, 0))
    hgroups = heads // DN_HB
    return pl.pallas_call(
        functools.partial(_delta_kernel, nc=nc, seq=seq),
        grid=(bsz, heads),
        in_specs=[col(0), col(heads), col(2 * heads),
                  pl.BlockSpec((seq, DN_HB * dh), lambda b, h: (b, 3 * hgroups + h // DN_HB)),
                  cw(0), cw(heads), cw(2 * heads), rows, rows, scal, scal,
                  pl.BlockSpec((1, dh), lambda b, h: (0, 0))],
        out_specs=pl.BlockSpec((seq, DN_HB * dh), lambda b, h: (b, h // DN_HB)),
        out_shape=jax.ShapeDtypeStruct((bsz * seq, heads * dh), BF16),
        scratch_shapes=[pltpu.VMEM((DN_HB, seq, dh), F32), pltpu.VMEM((DN_HB, seq, dh), F32),
                        pltpu.VMEM((DN_HB, seq, dh), F32),
                        pltpu.VMEM((DN_HB, nc, LANES), F32), pltpu.VMEM((DN_HB, nc, CHUNK), F32),
                        pltpu.VMEM((DN_HB, seq, dh), F32),
                        pltpu.VMEM((DN_HB * nc, 2 * CHUNK, dh), BF16), pltpu.VMEM((DN_HB * nc, CHUNK + dh, CHUNK), BF16),
                        pltpu.VMEM((DN_HB, nc, LANES), F32), pltpu.VMEM((DN_HB, dh, dh), F32),
                        pltpu.VMEM((DN_HB * nc * CHUNK, LANES), F32), pltpu.VMEM((CHUNK, CHUNK, LANES), F32),
                        pltpu.VMEM((CHUNK, CHUNK, LANES), F32)],
        compiler_params=_params("arbitrary", "arbitrary"),
        name="gated_deltanet",
    )(proj, proj, proj, proj, conv_w, conv_w, conv_w, bh[1], bh[0],
      a_log.reshape(heads, 1, 1), dt_bias.reshape(heads, 1, 1), norm_g.reshape(1, dh))


def _cexp(re, im):
    m = jnp.exp(re)
    return m * jnp.cos(im), m * jnp.sin(im)


def _cmul(ar, ai, br, bi):
    return ar * br - ai * bi, ar * bi + ai * br


def _gelu_tanh(x):
    return 0.5 * x * (1.0 + jnp.tanh(math.sqrt(2.0 / math.pi) * (x + 0.044715 * (x * x * x))))


S5_GB = LANES // S5_GROUP


def _s5_kernel(u_ref, bre_ref, bim_ref, cre_ref, cim_ref, lr_r_ref, li_r_ref, dt_r_ref,
               lr_c_ref, li_c_ref, dt_c_ref, d_ref, o_ref, ua_s, bps_s, cq_s, ks_s, y_s, *, nc):
    step, grp, st = S5_STEP, S5_GROUP, S5_STATE
    gw = S5_GB * grp
    half = S5_GB * st
    sw = 2 * half
    r = ua_s.shape[0]
    gsh, ssh = grp.bit_length() - 1, st.bit_length() - 1

    lr, li = lr_r_ref[0], li_r_ref[0]
    dt = jnp.exp(dt_r_ref[0])
    ar, ai = _cexp(lr * dt, li * dt)
    den = lr * lr + li * li
    pr = ((ar - 1.0) * lr + ai * li) / den
    pi_ = (ai * lr - (ar - 1.0) * li) / den
    re_lane = lax.broadcasted_iota(jnp.int32, (1, sw), 1) < half
    row_g = lax.shift_right_logical(lax.broadcasted_iota(jnp.int32, (gw, sw), 0), gsh)
    lane_g = lax.shift_right_logical(lax.broadcasted_iota(jnp.int32, (gw, sw), 1) & (half - 1), ssh)
    same_g = row_g == lane_g
    br = jnp.where(same_g, jnp.concatenate([bre_ref[0]] * (sw // LANES), axis=1), 0.0)
    bi = jnp.where(same_g, jnp.concatenate([bim_ref[0]] * (sw // LANES), axis=1), 0.0)
    bp0 = None
    for k in range(step):
        blk = br * jnp.where(re_lane, pr, pi_) + bi * jnp.where(re_lane, -pi_, pr)
        if k == 0:
            bp0 = blk
        s = step - 1 - k
        bps_s[s * gw:(s + 1) * gw, :] = blk.astype(BF16)
        pr, pi_ = _cmul(pr, pi_, ar, ai)

    dtc = jnp.exp(dt_c_ref[0])
    acr_d, aci_d = _cexp(lr_c_ref[0] * dtc, li_c_ref[0] * dtc)
    hr = half // LANES

    def spread(v):
        vt = jnp.concatenate([v, jnp.zeros((LANES - hr, LANES), F32)], axis=0).T
        return jnp.concatenate([jnp.broadcast_to(vt[:, q:q + 1], (LANES, gw)) for q in range(hr)], axis=0)

    acr, aci = spread(acr_d), spread(aci_d)
    same_c = (lax.shift_right_logical(lax.broadcasted_iota(jnp.int32, (half, gw), 0), ssh)
              == lax.shift_right_logical(lax.broadcasted_iota(jnp.int32, (half, gw), 1), gsh))
    pcr = jnp.where(same_c, cre_ref[0], 0.0)
    pci = jnp.where(same_c, cim_ref[0], 0.0)
    bp_hi = bp0.astype(BF16)
    bp_lo = (bp0 - bp_hi.astype(F32)).astype(BF16)
    for k in range(step + 1):
        re_hi, im_hi = pcr.astype(BF16), (-pci).astype(BF16)
        cq_hi = jnp.concatenate([re_hi, im_hi], axis=0)
        if k < step:
            cq_lo = jnp.concatenate([(pcr - re_hi.astype(F32)).astype(BF16),
                                     (-pci - im_hi.astype(F32)).astype(BF16)], axis=0)
            taps = _dot(bp_hi, cq_hi) + (_dot(bp_hi, cq_lo) + _dot(bp_lo, cq_hi))
            ks_s[(step - 1 - k) * gw:(step - k) * gw, :] = taps.astype(BF16)
            pcr, pci = _cmul(pcr, pci, acr, aci)
        if k >= 1:
            cq_s[k - 1] = cq_hi

    for s in range(step):
        ua_s[:, s * gw:(s + 1) * gw] = u_ref[pl.ds(s, r, stride=step), :].astype(BF16)
    x = _dot(ua_s[...], bps_s[...])
    blk_i = lax.broadcasted_iota(jnp.int32, (r, sw), 0) & (nc - 1)
    er, ei = ar, ai
    for _ in range(step.bit_length() - 1):
        er, ei = _cmul(er, ei, er, ei)
    sh = 1
    while sh < nc:
        xs = jnp.where(blk_i >= sh, pltpu.roll(x, sh, axis=0), 0.0)
        x = x + xs * er + pltpu.roll(xs, half, axis=1) * jnp.where(re_lane, -ei, ei)
        er, ei = _cmul(er, ei, er, ei)
        sh *= 2
    xp = jnp.where(blk_i >= 1, pltpu.roll(x, 1, axis=0), 0.0).astype(BF16)

    d = d_ref[0]
    for t in range(step):
        y = (_dot(ua_s[:, :(t + 1) * gw], ks_s[(step - 1 - t) * gw:, :]) + _dot(xp, cq_s[t])
             + d * u_ref[pl.ds(t, r, stride=step), :])
        y_s[pl.ds(t, r, stride=step), :] = _gelu_tanh(y)

    rt = 512

    def emit(i, carry):
        r0 = pl.multiple_of(i * rt, rt)
        o_ref[pl.ds(r0, rt), :] = y_s[pl.ds(r0, rt), :].astype(o_ref.dtype)
        return carry

    lax.fori_loop(0, y_s.shape[0] // rt, emit, 0)


def _s5(u2, a_re, a_im, b_re, b_im, c_re, c_im, d, log_dt, seq):
    groups, st = a_re.shape
    grp, step = S5_GROUP, S5_STEP
    m = u2.shape[0]
    g8 = groups // S5_GB
    gw, half = S5_GB * grp, S5_GB * st
    sw = 2 * half
    nc = seq // step
    assert nc & (nc - 1) == 0 and gw == LANES
    r = m // step
    bt = lambda t: jnp.tile(t.transpose(0, 2, 1).reshape(g8, gw, st), (1, 1, 2))
    ct = lambda t: jnp.tile(t.transpose(0, 2, 1).reshape(g8, half, grp), (1, 1, S5_GB))
    rowv = lambda t: jnp.tile(t.reshape(g8, 1, half), (1, 1, 2))
    colv = lambda t: t.reshape(g8, half // LANES, LANES)
    ldt = jnp.broadcast_to(log_dt[:, None], (groups, st))
    g3 = lambda shp: pl.BlockSpec((1,) + shp, lambda g: (g, 0, 0))
    dense = g3((half // LANES, LANES))
    return pl.pallas_call(
        functools.partial(_s5_kernel, nc=nc),
        grid=(g8,),
        in_specs=[pl.BlockSpec((m, gw), lambda g: (0, g)),
                  g3((gw, 2 * st)), g3((gw, 2 * st)), g3((half, gw)), g3((half, gw)),
                  g3((1, sw)), g3((1, sw)), g3((1, sw)), dense, dense, dense,
                  g3((1, gw))],
        out_specs=pl.BlockSpec((m, gw), lambda g: (0, g)),
        out_shape=jax.ShapeDtypeStruct((m, groups * grp), BF16),
        scratch_shapes=[pltpu.VMEM((r, step * gw), BF16), pltpu.VMEM((step * gw, sw), BF16),
                        pltpu.VMEM((step, sw, gw), BF16), pltpu.VMEM((step * gw, gw), BF16),
                        pltpu.VMEM((m, gw), F32)],
        compiler_params=_params("arbitrary"),
        name="s5_ssm",
    )(u2, bt(b_re), bt(b_im), ct(c_re), ct(c_im), rowv(a_re), rowv(a_im), rowv(ldt),
      colv(a_re), colv(a_im), colv(ldt), d.reshape(g8, 1, gw))


def _sconv_kernel(gb_ref, gc_ref, xc_ref, w_ref, o_ref):
    p = gc_ref[...].astype(F32) * xc_ref[...].astype(F32)
    w = w_ref[...]
    row = lax.broadcasted_iota(jnp.int32, p.shape, 0)
    acc = p * w[C_CONV - 1:C_CONV, :]
    for s in range(1, C_CONV):
        acc = acc + _shift_rows(p, s, row) * w[C_CONV - 1 - s:C_CONV - s, :]
    o_ref[...] = (gb_ref[...].astype(F32) * acc).astype(o_ref.dtype)


def _sconv(proj, conv_w, bsz, seq):
    width = conv_w.shape[1]
    tc = 256
    nb = width // tc
    col = lambda off: pl.BlockSpec((seq, tc), lambda b, j, off=off: (b, off * nb + j))
    return pl.pallas_call(
        _sconv_kernel,
        grid=(bsz, nb),
        in_specs=[col(0), col(1), col(2), pl.BlockSpec((C_CONV, tc), lambda b, j: (0, j))],
        out_specs=pl.BlockSpec((seq, tc), lambda b, j: (b, j)),
        out_shape=jax.ShapeDtypeStruct((bsz * seq, width), BF16),
        compiler_params=_params("arbitrary", "arbitrary"),
        name="short_conv",
    )(proj, proj, proj, conv_w)


NEG_BIG = -1e30


def _dattn_kernel(q_ref, k_ref, vt_ref, lq1_ref, lk1_ref, lq2_ref, lk2_ref, ng_ref, o_ref,
                  acc_s, *, heads, tq, lambda_init):
    dh = DA_HEAD_DIM
    qi = pl.program_id(1)
    scale = dh ** -0.5
    lam = (jnp.exp(jnp.sum(lq1_ref[...] * lk1_ref[...], axis=-1, keepdims=True))
           - jnp.exp(jnp.sum(lq2_ref[...] * lk2_ref[...], axis=-1, keepdims=True)) + lambda_init)
    ki = lax.broadcasted_iota(jnp.int32, (tq, tq), 0)
    qj = lax.broadcasted_iota(jnp.int32, (tq, tq), 1)
    rel = (qj - ki).astype(F32)
    visible = ki <= qj
    ng = ng_ref[...]

    hpi = 2
    for h0 in range(0, heads, hpi):
        slopes = [2.0 ** (-8.0 * (h + 1) / heads) for h in range(h0, h0 + hpi)]

        def block(j, carry, masked, h0=h0, slopes=slopes):
            k0 = pl.multiple_of(j * tq, tq)
            off = ((qi - j) * tq).astype(F32)
            out = []
            for t in range(hpi):
                v_lo = (h0 + t) * 2 * dh
                vt = vt_ref[0, v_lo:v_lo + 2 * dh, pl.ds(k0, tq)]
                bias = -slopes[t] * (rel + off)
                for mp in range(2):
                    c = 2 * t + mp
                    m, l = carry[2 * c], carry[2 * c + 1]
                    lo = v_lo + mp * dh
                    s = _dot_nt(k_ref[pl.ds(k0, tq), lo:lo + dh], q_ref[:, lo:lo + dh]) * scale + bias
                    if masked:
                        s = jnp.where(visible, s, NEG_BIG)
                    m_new = jnp.maximum(m, jnp.max(s, axis=0, keepdims=True))
                    corr = jnp.exp(m - m_new)
                    p = jnp.exp(s - m_new)
                    out += [m_new, corr * l + jnp.sum(p, axis=0, keepdims=True)]
                    acc_s[c] = corr * acc_s[c] + _dot(vt, p.astype(BF16))
            return tuple(out)

        acc_s[...] = jnp.zeros(acc_s.shape, F32)
        neg = jnp.full((1, tq), NEG_BIG, F32)
        zero = jnp.zeros((1, tq), F32)
        carry = lax.fori_loop(0, qi, functools.partial(block, masked=False), (neg, zero) * (2 * hpi))
        carry = block(qi, carry, True)
        for t in range(hpi):
            v_lo = (h0 + t) * 2 * dh
            l1, l2 = carry[4 * t + 1], carry[4 * t + 3]
            o = acc_s[2 * t] / l1 - lam * (acc_s[2 * t + 1] / l2)
            o = o * lax.rsqrt(jnp.mean(o * o, axis=0, keepdims=True) + EPS)
            o_ref[:, v_lo:v_lo + 2 * dh] = (o.T * ng * (1.0 - lambda_init)).astype(o_ref.dtype)


def _dattn(proj, vt, lq1, lk1, lq2, lk2, norm_g, lambda_init, bsz, seq):
    dh = DA_HEAD_DIM
    width = vt.shape[1]
    heads = width // (2 * dh)
    tq = min(256, seq)
    nq = seq // tq
    vec = pl.BlockSpec((1, dh), lambda b, i: (0, 0))
    return pl.pallas_call(
        functools.partial(_dattn_kernel, heads=heads, tq=tq, lambda_init=lambda_init),
        grid=(bsz, nq),
        in_specs=[pl.BlockSpec((tq, width), lambda b, i: (b * nq + i, 3)),
                  pl.BlockSpec((seq, width), lambda b, i: (b, 4)),
                  pl.BlockSpec((1, width, seq), lambda b, i: (b, 0, 0)),
                  vec, vec, vec, vec, pl.BlockSpec((1, 2 * dh), lambda b, i: (0, 0))],
        out_specs=pl.BlockSpec((tq, width), lambda b, i: (b * nq + i, 0)),
        out_shape=jax.ShapeDtypeStruct((bsz * seq, width), BF16),
        scratch_shapes=[pltpu.VMEM((4, 2 * dh, tq), F32)],
        compiler_params=_params("arbitrary", "arbitrary"),
        name="diff_attention",
    )(proj, proj, vt, lq1.reshape(1, dh), lk1.reshape(1, dh), lq2.reshape(1, dh), lk2.reshape(1, dh),
      norm_g.reshape(1, 2 * dh))


def kernel(x, c, ada_w, ada_b, ln1_g, ln1_b, ln2_g, ln2_b, ffn_w_gate, ffn_w_up, ffn_w_down, ab_w_in, ab_w_out, dn_conv_w, dn_a_log, dn_dt_bias, dn_norm_g, s5_a_re, s5_a_im, s5_b_re, s5_b_im, s5_c_re, s5_c_im, s5_d, s5_log_dt, s5_w_glu, cd_w_in, cd_w_out, sc_conv_w, da_lq1, da_lk1, da_lq2, da_lk2, da_norm_g):
    bsz, seq, d = x.shape
    depth = ada_w.shape[0]
    m = bsz * seq
    a_width = dn_conv_w.shape[-1] // 3
    heads = dn_a_log.shape[-1]

    c_pad = jnp.zeros((8, d), F32).at[:bsz].set(c)
    mod = _ada_mod(c_pad, ada_w, ada_b)

    def mvec(i, j):
        return mod[i, :bsz, j * d:(j + 1) * d].reshape(bsz, 1, d)

    x2 = x.reshape(m, d)
    h = _modulate(x2, mvec(0, 1), mvec(0, 0), seq)
    for i in range(depth):
        j = i // 2
        if i % 2 == 0:
            w_uba = jnp.pad(jnp.concatenate([ab_w_in[j:j + 1, :, 4 * a_width + 2 * heads:],
                                             ab_w_in[j:j + 1, :, 4 * a_width:4 * a_width + 2 * heads]], axis=2),
                            ((0, 0), (0, 0), (0, LANES - 2 * heads)))
            b_width = w_uba.shape[2] - LANES
            proj = _mm(h, ab_w_in, j, 4 * a_width, BF16, 1024, 1024, "ab_in_proj")
            uf = _mm(h, w_uba, 0, w_uba.shape[2], F32, 1024, w_uba.shape[2], "ab_u_proj")
            ya = _deltanet(proj, uf[:, b_width:], dn_conv_w[j], dn_a_log[j], dn_dt_bias[j], dn_norm_g[j], bsz, seq)
            yb = _s5(uf, s5_a_re[j], s5_a_im[j], s5_b_re[j], s5_b_im[j],
                     s5_c_re[j], s5_c_im[j], s5_d[j], s5_log_dt[j], seq)
            yb = _glu(yb, s5_w_glu[j].astype(BF16), 1024)
            w_out = ab_w_out[j].astype(BF16)
        else:
            lambda_init = 0.8 - 0.6 * math.exp(-0.3 * i)
            c_width = sc_conv_w.shape[-1]
            proj = _mm(h, cd_w_in, j, 5 * c_width, BF16, 1024, 1024, "cd_in_proj")
            vt = _mm_t(h, cd_w_in[j, :, 5 * c_width:].T, bsz, seq, 512, 1024, "cd_value_proj")
            ya = _sconv(proj, sc_conv_w[j], bsz, seq)
            yb = _dattn(proj, vt, da_lq1[j], da_lk1[j], da_lq2[j], da_lk2[j], da_norm_g[j], lambda_init, bsz, seq)
            w_out = cd_w_out[j].astype(BF16)
        x2, h = _res_ln([ya, yb], w_out, x2, mvec(i, 2), ln1_g[i], ln1_b[i],
                        (mvec(i, 4), mvec(i, 3)), seq, 512, "mixer_out_ln")
        act = _gate_up(h, ffn_w_gate, ffn_w_up, i, 2048, 512)
        nxt = (mvec(i + 1, 1), mvec(i + 1, 0)) if i + 1 < depth else None
        x2, h = _res_ln([act], ffn_w_down[i].astype(BF16), x2, mvec(i, 5), ln2_g[i], ln2_b[i],
                        nxt, seq, 256, "ffn_down_ln")
    return x2.reshape(bsz, seq, d)
```

```python
import functools
import math

import jax
import jax.numpy as jnp
from jax import lax
from jax.experimental import pallas as pl
from jax.experimental.pallas import tpu as pltpu

F32 = jnp.float32
BF16 = jnp.bfloat16

DEPTH = 4
A_HEAD_DIM = 128
A_CONV = 4
CHUNK = 64
S5_GROUP = 16
S5_STATE = 64
S5_STEP = 16
C_CONV = 3
DA_HEAD_DIM = 128
ALPHA = (2.0 * DEPTH) ** 0.25
EPS = 1e-5
LANES = 128
VMEM_LIMIT_BYTES = 56 * 1024 * 1024


def _params(*sem):
    return pltpu.CompilerParams(dimension_semantics=sem, vmem_limit_bytes=VMEM_LIMIT_BYTES)


def _silu(x):
    return x * jax.nn.sigmoid(x)


def _dot(a, b):
    return jnp.dot(a, b, preferred_element_type=F32)


def _dot_nt(a, b):
    return lax.dot_general(a, b, (((1,), (1,)), ((), ())), preferred_element_type=F32)


def _dot_tn(a, b):
    return lax.dot_general(a, b, (((0,), (0,)), ((), ())), preferred_element_type=F32)


def _split3(x):
    hi = x.astype(BF16)
    r1 = x - hi.astype(F32)
    mid = r1.astype(BF16)
    lo = (r1 - mid.astype(F32)).astype(BF16)
    return hi, mid, lo


def _dot_f32(a, b):
    a0, a1, a2 = _split3(a)
    b0, b1, b2 = _split3(b)
    return (_dot(a0, b0) + (_dot(a0, b1) + _dot(a1, b0))
            + (_dot(a1, b1) + _dot(a0, b2) + _dot(a2, b0)))


def _ada_kernel(c_ref, w_ref, b_ref, o_ref):
    c = c_ref[...]
    ca = _silu(c).astype(BF16)
    o_ref[0] = _dot(ca, w_ref[0].astype(BF16)) + b_ref[0]


def _ada_mod(c_pad, ada_w, ada_b):
    depth, d, n = ada_w.shape
    rows = c_pad.shape[0]
    tn = 1024
    return pl.pallas_call(
        _ada_kernel,
        grid=(depth, n // tn),
        in_specs=[pl.BlockSpec((rows, d), lambda i, j: (0, 0)),
                  pl.BlockSpec((1, d, tn), lambda i, j: (i, 0, j)),
                  pl.BlockSpec((1, 1, tn), lambda i, j: (i, 0, j))],
        out_specs=pl.BlockSpec((1, rows, tn), lambda i, j: (i, 0, j)),
        out_shape=jax.ShapeDtypeStruct((depth, rows, n), F32),
        compiler_params=_params("arbitrary", "arbitrary"),
        name="ada_mod",
    )(c_pad, ada_w, ada_b.reshape(depth, 1, n))


def _mod_spec(d, sel):
    layer, col = sel
    return pl.BlockSpec((1, 8, d), lambda *_: (layer, 0, col))


def _modulate_kernel(x_ref, sc_ref, sh_ref, o_ref, *, nl):
    b = pl.program_id(0) // nl
    o_ref[...] = (x_ref[...] * (1.0 + sc_ref[0, pl.ds(b, 1), :]) + sh_ref[0, pl.ds(b, 1), :]).astype(o_ref.dtype)


def _modulate(x2, mod, sc, sh, seq):
    m, d = x2.shape
    tl = min(seq, 512)
    return pl.pallas_call(
        functools.partial(_modulate_kernel, nl=seq // tl),
        grid=(m // tl,),
        in_specs=[pl.BlockSpec((tl, d), lambda i: (i, 0)), _mod_spec(d, sc), _mod_spec(d, sh)],
        out_specs=pl.BlockSpec((tl, d), lambda i: (i, 0)),
        out_shape=jax.ShapeDtypeStruct((m, d), BF16),
        compiler_params=_params("arbitrary"),
        name="modulate",
    )(x2, mod, mod)


def _mm_kernel(a_ref, w_ref, o_ref, w_s):
    @pl.when(pl.program_id(1) == 0)
    def _():
        w_s[...] = w_ref[0].astype(BF16)

    o_ref[...] = _dot(a_ref[...], w_s[...]).astype(o_ref.dtype)


def _mm(a, w, layer, n, out_dtype, tm, tn, name):
    m, k = a.shape
    tm = min(tm, m)
    tn = min(tn, n)
    return pl.pallas_call(
        _mm_kernel,
        grid=(n // tn, m // tm),
        in_specs=[pl.BlockSpec((tm, k), lambda j, i: (i, 0)),
                  pl.BlockSpec((1, k, tn), lambda j, i: (layer, 0, j))],
        out_specs=pl.BlockSpec((tm, tn), lambda j, i: (i, j)),
        out_shape=jax.ShapeDtypeStruct((m, n), out_dtype),
        scratch_shapes=[pltpu.VMEM((k, tn), BF16)],
        compiler_params=_params("arbitrary", "arbitrary"),
        name=name,
    )(a, w)


def _mm_t_kernel(wt_ref, a_ref, o_ref, w_s):
    @pl.when(pl.program_id(1) == 0)
    def _():
        w_s[...] = wt_ref[...].astype(BF16)

    o_ref[0] = _dot_nt(w_s[...], a_ref[...]).astype(o_ref.dtype)


def _mm_t(a, wt, bsz, seq, tn, tl, name):
    m, k = a.shape
    n = wt.shape[0]
    tl = min(tl, seq)
    nl = seq // tl
    return pl.pallas_call(
        _mm_t_kernel,
        grid=(n // tn, m // tl),
        in_specs=[pl.BlockSpec((tn, k), lambda j, i: (j, 0)),
                  pl.BlockSpec((tl, k), lambda j, i: (i, 0))],
        out_specs=pl.BlockSpec((1, tn, tl), lambda j, i: (i // nl, j, i % nl)),
        out_shape=jax.ShapeDtypeStruct((bsz, n, seq), BF16),
        scratch_shapes=[pltpu.VMEM((tn, k), BF16)],
        compiler_params=_params("arbitrary", "arbitrary"),
        name=name,
    )(wt, a)


def _gate_up_kernel(a_ref, wg_ref, wu_ref, o_ref, wg_s, wu_s):
    @pl.when(pl.program_id(1) == 0)
    def _():
        wg_s[...] = wg_ref[0].astype(BF16)
        wu_s[...] = wu_ref[0].astype(BF16)

    a = a_ref[...]
    g = _dot(a, wg_s[...])
    u = _dot(a, wu_s[...])
    o_ref[...] = (_silu(g) * u).astype(o_ref.dtype)


def _gate_up(a, wg, wu, layer, tm, tn):
    m, k = a.shape
    n = wg.shape[2]
    tm = min(tm, m)
    return pl.pallas_call(
        _gate_up_kernel,
        grid=(n // tn, m // tm),
        in_specs=[pl.BlockSpec((tm, k), lambda j, i: (i, 0)),
                  pl.BlockSpec((1, k, tn), lambda j, i: (layer, 0, j)),
                  pl.BlockSpec((1, k, tn), lambda j, i: (layer, 0, j))],
        out_specs=pl.BlockSpec((tm, tn), lambda j, i: (i, j)),
        out_shape=jax.ShapeDtypeStruct((m, n), BF16),
        scratch_shapes=[pltpu.VMEM((k, tn), BF16), pltpu.VMEM((k, tn), BF16)],
        compiler_params=_params("arbitrary", "arbitrary"),
        name="ffn_gate_up",
    )(a, wg, wu)


def _glu_kernel(y_ref, w_ref, o_ref):
    y = y_ref[...]
    t = _dot(y, w_ref[...])
    o_ref[...] = (y.astype(F32) * jax.nn.sigmoid(t)).astype(o_ref.dtype)


def _glu(y, w, tm):
    m, k = y.shape
    tm = min(tm, m)
    return pl.pallas_call(
        _glu_kernel,
        grid=(m // tm,),
        in_specs=[pl.BlockSpec((tm, k), lambda i: (i, 0)),
                  pl.BlockSpec((k, k), lambda i: (0, 0))],
        out_specs=pl.BlockSpec((tm, k), lambda i: (i, 0)),
        out_shape=jax.ShapeDtypeStruct((m, k), BF16),
        compiler_params=_params("arbitrary"),
        name="s5_glu",
    )(y, w)


def _res_ln_kernel(*refs, n_in, has_next, nl):
    a_refs = refs[:n_in]
    w_refs = refs[n_in:2 * n_in]
    x_ref, g_ref, lng_ref, lnb_ref = refs[2 * n_in:2 * n_in + 4]
    pos = 2 * n_in + 4
    if has_next:
        sc_ref, sh_ref = refs[pos:pos + 2]
        pos += 2
    xo_ref = refs[pos]
    pos += 1
    if has_next:
        ho_ref = refs[pos]
        pos += 1
    y_bufs = refs[pos:pos + 2]
    i = pl.program_id(0)
    b = jnp.maximum(i - 1, 0) // nl

    @pl.when(i == 0)
    def _():
        y_bufs[1][...] = jnp.zeros(y_bufs[1].shape, F32)

    def step(y_new, y_old):
        part = _dot(a_refs[0][...], w_refs[0][0])
        for a_ref, w_ref in zip(a_refs[1:], w_refs[1:]):
            part = part + _dot(a_ref[...], w_ref[0])
        y_new[...] = part
        r = ALPHA * x_ref[...] + (1.0 + g_ref[0, pl.ds(b, 1), :]) * y_old[...]
        mu = jnp.mean(r, axis=-1, keepdims=True)
        rc = r - mu
        var = jnp.mean(rc * rc, axis=-1, keepdims=True)
        xn = rc * lax.rsqrt(var + EPS) * lng_ref[...] + lnb_ref[...]
        xo_ref[...] = xn
        if has_next:
            ho_ref[...] = (xn * (1.0 + sc_ref[0, pl.ds(b, 1), :]) + sh_ref[0, pl.ds(b, 1), :]).astype(ho_ref.dtype)

    even = lax.rem(i, 2) == 0

    @pl.when(even)
    def _():
        step(y_bufs[0], y_bufs[1])

    @pl.when(jnp.logical_not(even))
    def _():
        step(y_bufs[1], y_bufs[0])


def _res_ln(a_list, w, layer, x2, mod, gate, ln_g, ln_b, nxt, seq, tm, name):
    m, d = x2.shape
    n_in = len(a_list)
    ka = a_list[0].shape[1]
    tm = min(tm, m, seq)
    nt = m // tm
    nl = seq // tm
    has_next = nxt is not None
    cur = lambda i: (jnp.minimum(i, nt - 1), 0)
    prev = lambda i: (jnp.maximum(i - 1, 0), 0)
    in_specs = [pl.BlockSpec((tm, ka), cur) for _ in a_list]
    in_specs += [pl.BlockSpec((1, ka, d), functools.partial(lambda i, s: (layer, s, 0), s=s),
                              pipeline_mode=pl.Buffered(1)) for s in range(n_in)]
    pvec = pl.BlockSpec((1, d), lambda i: (0, 0))
    in_specs += [pl.BlockSpec((tm, d), prev), _mod_spec(d, gate), pvec, pvec]
    args = list(a_list) + [w] * n_in + [x2, mod, ln_g.reshape(1, d), ln_b.reshape(1, d)]
    out_shape = [jax.ShapeDtypeStruct((m, d), F32)]
    out_specs = [pl.BlockSpec((tm, d), prev)]
    if has_next:
        in_specs += [_mod_spec(d, nxt[0]), _mod_spec(d, nxt[1])]
        args += [mod, mod]
        out_shape.append(jax.ShapeDtypeStruct((m, d), BF16))
        out_specs.append(pl.BlockSpec((tm, d), prev))
    out = pl.pallas_call(
        functools.partial(_res_ln_kernel, n_in=n_in, has_next=has_next, nl=nl),
        grid=(nt + 1,),
        in_specs=in_specs,
        out_specs=out_specs,
        out_shape=out_shape,
        scratch_shapes=[pltpu.VMEM((tm, d), F32), pltpu.VMEM((tm, d), F32)],
        compiler_params=_params("arbitrary"),
        name=name,
    )(*args)
    return (out[0], out[1]) if has_next else (out[0], None)


def _shift_rows(x, s, row):
    return jnp.where(row >= s, pltpu.roll(x, s, axis=0), 0.0)


def _conv_silu(x_ref, cw_ref):
    x = x_ref[...].astype(F32)
    w = cw_ref[...]
    row = lax.broadcasted_iota(jnp.int32, x.shape, 0)
    acc = x * w[A_CONV - 1:A_CONV, :]
    for s in range(1, A_CONV):
        acc = acc + _shift_rows(x, s, row) * w[A_CONV - 1 - s:A_CONV - s, :]
    return _silu(acc)


def _l2n(t):
    return t * lax.rsqrt(jnp.sum(t * t, axis=-1, keepdims=True) + 1e-6)


DN_HB = 4


def _delta_kernel(q_ref, k_ref, v_ref, z_ref, cwq_ref, cwk_ref, cwv_ref, a_ref, b_ref,
                  alog_ref, dtb_ref, ng_ref, o_ref,
                  q_s, k_s, v_s, gc_s, bt_s, u_s, wq_s, l2_s, gl_s, s_s, n_s, nt_s, x_s, *, nc, seq):
    c_len, dh = CHUNK, A_HEAD_DIM
    nmat_all = DN_HB * nc
    hh = lax.rem(pl.program_id(1), DN_HB)
    q_s[hh] = _l2n(_conv_silu(q_ref, cwq_ref)) * (dh ** -0.5)
    k_s[hh] = _l2n(_conv_silu(k_ref, cwk_ref))
    v_s[hh] = _conv_silu(v_ref, cwv_ref)

    a = a_ref[0, 0]
    b = b_ref[0, 0]
    sp_in = a + dtb_ref[0]
    softplus = jnp.maximum(sp_in, 0.0) + jnp.log1p(jnp.exp(-jnp.abs(sp_in)))
    g = -jnp.exp(alog_ref[0]) * softplus
    ri = lax.broadcasted_iota(jnp.int32, (c_len, LANES), 0)
    ci = lax.broadcasted_iota(jnp.int32, (c_len, LANES), 1)
    tril, strict, eye = ri >= ci, ri > ci, ri == ci
    eye64 = eye[:, :c_len]
    eye_bf = jnp.where(eye64, 1.0, 0.0).astype(BF16)
    g0, g1, g2 = _split3(g)
    triu = jnp.where(ci < c_len, jnp.where(ri <= ci, 1.0, 0.0), 0.0).astype(BF16)
    gc_s[hh] = _dot(g0, triu) + _dot(g1, triu) + _dot(g2, triu)
    bt_s[hh] = jax.nn.sigmoid(b)

    def prep(c, carry):
        r0 = pl.multiple_of(c * c_len, c_len)
        for hd in range(DN_HB):
            kc = k_s[hd, pl.ds(r0, c_len), :]
            qc = q_s[hd, pl.ds(r0, c_len), :]
            grow = gc_s[hd, pl.ds(c, 1), :]
            brow = bt_s[hd, pl.ds(c, 1), :]
            gcol = jnp.sum(jnp.where(eye, grow, 0.0), axis=1, keepdims=True)
            bcol = jnp.sum(jnp.where(eye64, brow, 0.0), axis=1, keepdims=True)
            decay = jnp.where(tril, jnp.exp(jnp.where(tril, gcol - grow, 0.0)), 0.0)
            kb = kc.astype(BF16)
            kk = _dot_nt(kb, jnp.concatenate([kb, jnp.zeros_like(kb)], axis=0))
            m0 = pl.multiple_of((hd * nc + c) * c_len, c_len)
            n_s[pl.ds(m0, c_len), :] = jnp.where(strict, kk * bcol * decay, 0.0)
            qk = _dot_nt(qc.astype(BF16), kb)
            glast = grow[:, c_len - 1:c_len]
            kt = (kc * jnp.exp(glast - gcol)).astype(BF16)
            wq_s[hd * nc + c, c_len:, :] = (qc * jnp.exp(gcol)).astype(BF16)
            l2_s[hd * nc + c, :c_len, :] = (qk * decay[:, :c_len]).astype(BF16)
            l2_s[hd * nc + c, c_len:, :] = _dot_tn(kt, eye_bf).astype(BF16)
            gl_s[hd, pl.ds(c, 1), :] = jnp.broadcast_to(jnp.exp(glast), (1, LANES))
        return carry

    @pl.when(hh == DN_HB - 1)
    def _():
        lax.fori_loop(0, nc, prep, 0)
        x_s[...] = jnp.zeros(x_s.shape, F32)
        for i in range(c_len):
            slab = n_s[pl.ds(i, nmat_all, stride=c_len), :]
            nt_s[i] = slab.T[:c_len, :]
        sub = lax.broadcasted_iota(jnp.int32, (8, LANES), 0)
        for ib in range(c_len // 8):
            npc = ib + 1

            def row(ii, carry, ib=ib, npc=npc):
                i = ib * 8 + ii
                parts = [[None] * 4 for _ in range(npc)]
                for j in range(8 * npc):
                    coef = nt_s[i, pl.ds(j, 1), :]
                    for p in range(j // 8 + 1):
                        term = coef * x_s[j, p * 8:(p + 1) * 8, :]
                        k = j % 4
                        parts[p][k] = term if parts[p][k] is None else parts[p][k] + term
                for p in range(npc):
                    live = [t for t in parts[p] if t is not None]
                    tot = live[0]
                    for t in live[1:]:
                        tot = tot + t
                    x_s[i, p * 8:(p + 1) * 8, :] = jnp.where(sub + p * 8 == i, 1.0, 0.0) - tot
                return carry

            lax.fori_loop(0, 8, row, 0)
        zpad = jnp.zeros((LANES - c_len, LANES), F32)
        for i in range(c_len):
            n_s[pl.ds(i, nmat_all, stride=c_len), :] = jnp.concatenate([x_s[i], zpad], axis=0).T

        def uw(hd, c):
            r0 = pl.multiple_of(c * c_len, c_len)
            idx = hd * nc + c
            t = n_s[pl.ds(pl.multiple_of(idx * c_len, c_len), c_len), :][:, :c_len]
            tb = t * bt_s[hd, pl.ds(c, 1), :]
            tw = tb * jnp.exp(gc_s[hd, pl.ds(c, 1), :][:, :c_len])
            kb = k_s[hd, pl.ds(r0, c_len), :].astype(BF16)
            vb = v_s[hd, pl.ds(r0, c_len), :].astype(BF16)
            u_s[hd, pl.ds(r0, c_len), :] = _dot(tb.astype(BF16), vb)
            wq_s[idx, :c_len, :] = _dot(tw.astype(BF16), kb).astype(BF16)

        for hd in range(DN_HB):
            uw(hd, 0)
        ng = ng_ref[...]
        s_s[...] = jnp.zeros(s_s.shape, F32)

        def scan(c, carry):
            r0 = pl.multiple_of(c * c_len, c_len)
            for hd in range(DN_HB):
                s = s_s[hd]
                sb = s.astype(BF16)
                r1 = _dot(wq_s[hd * nc + c], sb)
                v_new = u_s[hd, pl.ds(r0, c_len), :] - r1[:c_len]
                r2 = _dot(l2_s[hd * nc + c], v_new.astype(BF16))
                o = r1[c_len:] + r2[:c_len]
                s_s[hd] = s * gl_s[hd, pl.ds(c, 1), :] + r2[c_len:]
                q_s[hd, pl.ds(r0, c_len), :] = o * lax.rsqrt(jnp.mean(o * o, axis=-1, keepdims=True) + EPS) * ng
            cn = jnp.minimum(c + 1, nc - 1)
            for hd in range(DN_HB):
                uw(hd, cn)
            return carry

        lax.fori_loop(0, nc, scan, 0)

        rt = 256

        def gate(r, carry):
            r0 = pl.multiple_of(r * rt, rt)
            for hd in range(DN_HB):
                zc = z_ref[pl.ds(r0, rt), hd * dh:(hd + 1) * dh].astype(F32)
                o_ref[pl.ds(r0, rt), hd * dh:(hd + 1) * dh] = (q_s[hd, pl.ds(r0, rt), :] * _silu(zc)).astype(o_ref.dtype)
            return carry

        lax.fori_loop(0, seq // rt, gate, 0)


def _deltanet(proj, ba, conv_w, a_log, dt_bias, norm_g, bsz, seq):
    dh = A_HEAD_DIM
    heads = a_log.shape[0]
    nc = seq // CHUNK
    assert DN_HB * nc == LANES and heads % DN_HB == 0
    bh = ba[:, :2 * heads].reshape(bsz, seq, 2, heads).transpose(2, 0, 3, 1).reshape(2, bsz, heads, nc, CHUNK)
    col = lambda off: pl.BlockSpec((seq, dh), lambda b, h, off=off: (b, off + h))
    cw = lambda off: pl.BlockSpec((A_CONV, dh), lambda b, h, off=off: (0, off + h))
    rows = pl.BlockSpec((1, 1, nc, CHUNK), lambda b, h: (b, h, 0, 0))
    scal = pl.BlockSpec((1, 1, 1), lambda b, h: (h, 0, 0))
    hgroups = heads // DN_HB
    return pl.pallas_call(
        functools.partial(_delta_kernel, nc=nc, seq=seq),
        grid=(bsz, heads),
        in_specs=[col(0), col(heads), col(2 * heads),
                  pl.BlockSpec((seq, DN_HB * dh), lambda b, h: (b, 3 * hgroups + h // DN_HB)),
                  cw(0), cw(heads), cw(2 * heads), rows, rows, scal, scal,
                  pl.BlockSpec((1, dh), lambda b, h: (0, 0))],
        out_specs=pl.BlockSpec((seq, DN_HB * dh), lambda b, h: (b, h // DN_HB)),
        out_shape=jax.ShapeDtypeStruct((bsz * seq, heads * dh), BF16),
        scratch_shapes=[pltpu.VMEM((DN_HB, seq, dh), F32), pltpu.VMEM((DN_HB, seq, dh), F32),
                        pltpu.VMEM((DN_HB, seq, dh), F32),
                        pltpu.VMEM((DN_HB, nc, LANES), F32), pltpu.VMEM((DN_HB, nc, CHUNK), F32),
                        pltpu.VMEM((DN_HB, seq, dh), F32),
                        pltpu.VMEM((DN_HB * nc, 2 * CHUNK, dh), BF16), pltpu.VMEM((DN_HB * nc, CHUNK + dh, CHUNK), BF16),
                        pltpu.VMEM((DN_HB, nc, LANES), F32), pltpu.VMEM((DN_HB, dh, dh), F32),
                        pltpu.VMEM((DN_HB * nc * CHUNK, LANES), F32), pltpu.VMEM((CHUNK, CHUNK, LANES), F32),
                        pltpu.VMEM((CHUNK, CHUNK, LANES), F32)],
        compiler_params=_params("arbitrary", "arbitrary"),
        name="gated_deltanet",
    )(proj, proj, proj, proj, conv_w, conv_w, conv_w, bh[1], bh[0],
      a_log.reshape(heads, 1, 1), dt_bias.reshape(heads, 1, 1), norm_g.reshape(1, dh))


def _cexp(re, im):
    m = jnp.exp(re)
    return m * jnp.cos(im), m * jnp.sin(im)


def _cmul(ar, ai, br, bi):
    return ar * br - ai * bi, ar * bi + ai * br


def _gelu_tanh(x):
    return 0.5 * x * (1.0 + jnp.tanh(math.sqrt(2.0 / math.pi) * (x + 0.044715 * (x * x * x))))


S5_GB = LANES // S5_GROUP


def _s5_kernel(u_ref, bre_ref, bim_ref, cre_ref, cim_ref, lr_r_ref, li_r_ref, dt_r_ref,
               lr_c_ref, li_c_ref, dt_c_ref, d_ref, o_ref, ua_s, bps_s, cq_s, ks_s, y_s, *, nc):
    step, grp, st = S5_STEP, S5_GROUP, S5_STATE
    gw = S5_GB * grp
    half = S5_GB * st
    sw = 2 * half
    r = ua_s.shape[0]
    gsh, ssh = grp.bit_length() - 1, st.bit_length() - 1

    lr, li = lr_r_ref[0], li_r_ref[0]
    dt = jnp.exp(dt_r_ref[0])
    ar, ai = _cexp(lr * dt, li * dt)
    den = lr * lr + li * li
    pr = ((ar - 1.0) * lr + ai * li) / den
    pi_ = (ai * lr - (ar - 1.0) * li) / den
    re_lane = lax.broadcasted_iota(jnp.int32, (1, sw), 1) < half
    row_g = lax.shift_right_logical(lax.broadcasted_iota(jnp.int32, (gw, sw), 0), gsh)
    lane_g = lax.shift_right_logical(lax.broadcasted_iota(jnp.int32, (gw, sw), 1) & (half - 1), ssh)
    same_g = row_g == lane_g
    br = jnp.where(same_g, jnp.concatenate([bre_ref[0]] * (sw // LANES), axis=1), 0.0)
    bi = jnp.where(same_g, jnp.concatenate([bim_ref[0]] * (sw // LANES), axis=1), 0.0)
    bp0 = None
    for k in range(step):
        blk = br * jnp.where(re_lane, pr, pi_) + bi * jnp.where(re_lane, -pi_, pr)
        if k == 0:
            bp0 = blk
        s = step - 1 - k
        bps_s[s * gw:(s + 1) * gw, :] = blk.astype(BF16)
        pr, pi_ = _cmul(pr, pi_, ar, ai)

    dtc = jnp.exp(dt_c_ref[0])
    acr_d, aci_d = _cexp(lr_c_ref[0] * dtc, li_c_ref[0] * dtc)
    hr = half // LANES

    def spread(v):
        vt = jnp.concatenate([v, jnp.zeros((LANES - hr, LANES), F32)], axis=0).T
        return jnp.concatenate([jnp.broadcast_to(vt[:, q:q + 1], (LANES, gw)) for q in range(hr)], axis=0)

    acr, aci = spread(acr_d), spread(aci_d)
    same_c = (lax.shift_right_logical(lax.broadcasted_iota(jnp.int32, (half, gw), 0), ssh)
              == lax.shift_right_logical(lax.broadcasted_iota(jnp.int32, (half, gw), 1), gsh))
    pcr = jnp.where(same_c, cre_ref[0], 0.0)
    pci = jnp.where(same_c, cim_ref[0], 0.0)
    bp_hi = bp0.astype(BF16)
    bp_lo = (bp0 - bp_hi.astype(F32)).astype(BF16)
    for k in range(step + 1):
        re_hi, im_hi = pcr.astype(BF16), (-pci).astype(BF16)
        cq_hi = jnp.concatenate([re_hi, im_hi], axis=0)
        if k < step:
            cq_lo = jnp.concatenate([(pcr - re_hi.astype(F32)).astype(BF16),
                                     (-pci - im_hi.astype(F32)).astype(BF16)], axis=0)
            taps = _dot(bp_hi, cq_hi) + (_dot(bp_hi, cq_lo) + _dot(bp_lo, cq_hi))
            ks_s[(step - 1 - k) * gw:(step - k) * gw, :] = taps.astype(BF16)
            pcr, pci = _cmul(pcr, pci, acr, aci)
        if k >= 1:
            cq_s[k - 1] = cq_hi

    for s in range(step):
        ua_s[:, s * gw:(s + 1) * gw] = u_ref[pl.ds(s, r, stride=step), :].astype(BF16)
    x = _dot(ua_s[...], bps_s[...])
    blk_i = lax.broadcasted_iota(jnp.int32, (r, sw), 0) & (nc - 1)
    er, ei = ar, ai
    for _ in range(step.bit_length() - 1):
        er, ei = _cmul(er, ei, er, ei)
    sh = 1
    while sh < nc:
        xs = jnp.where(blk_i >= sh, pltpu.roll(x, sh, axis=0), 0.0)
        x = x + xs * er + pltpu.roll(xs, half, axis=1) * jnp.where(re_lane, -ei, ei)
        er, ei = _cmul(er, ei, er, ei)
        sh *= 2
    xp = jnp.where(blk_i >= 1, pltpu.roll(x, 1, axis=0), 0.0).astype(BF16)

    d = d_ref[0]
    for t in range(step):
        y = (_dot(ua_s[:, :(t + 1) * gw], ks_s[(step - 1 - t) * gw:, :]) + _dot(xp, cq_s[t])
             + d * u_ref[pl.ds(t, r, stride=step), :])
        y_s[pl.ds(t, r, stride=step), :] = _gelu_tanh(y)

    rt = 512

    def emit(i, carry):
        r0 = pl.multiple_of(i * rt, rt)
        o_ref[pl.ds(r0, rt), :] = y_s[pl.ds(r0, rt), :].astype(o_ref.dtype)
        return carry

    lax.fori_loop(0, y_s.shape[0] // rt, emit, 0)


def _s5(u2, a_re, a_im, b_re, b_im, c_re, c_im, d, log_dt, seq):
    groups, st = a_re.shape
    grp, step = S5_GROUP, S5_STEP
    m = u2.shape[0]
    g8 = groups // S5_GB
    gw, half = S5_GB * grp, S5_GB * st
    sw = 2 * half
    nc = seq // step
    assert nc & (nc - 1) == 0 and gw == LANES
    r = m // step
    bt = lambda t: jnp.tile(t.transpose(0, 2, 1).reshape(g8, gw, st), (1, 1, 2))
    ct = lambda t: jnp.tile(t.transpose(0, 2, 1).reshape(g8, half, grp), (1, 1, S5_GB))
    rowv = lambda t: jnp.tile(t.reshape(g8, 1, half), (1, 1, 2))
    colv = lambda t: t.reshape(g8, half // LANES, LANES)
    ldt = jnp.broadcast_to(log_dt[:, None], (groups, st))
    g3 = lambda shp: pl.BlockSpec((1,) + shp, lambda g: (g, 0, 0))
    dense = g3((half // LANES, LANES))
    return pl.pallas_call(
        functools.partial(_s5_kernel, nc=nc),
        grid=(g8,),
        in_specs=[pl.BlockSpec((m, gw), lambda g: (0, g)),
                  g3((gw, 2 * st)), g3((gw, 2 * st)), g3((half, gw)), g3((half, gw)),
                  g3((1, sw)), g3((1, sw)), g3((1, sw)), dense, dense, dense,
                  g3((1, gw))],
        out_specs=pl.BlockSpec((m, gw), lambda g: (0, g)),
        out_shape=jax.ShapeDtypeStruct((m, groups * grp), BF16),
        scratch_shapes=[pltpu.VMEM((r, step * gw), BF16), pltpu.VMEM((step * gw, sw), BF16),
                        pltpu.VMEM((step, sw, gw), BF16), pltpu.VMEM((step * gw, gw), BF16),
                        pltpu.VMEM((m, gw), F32)],
        compiler_params=_params("arbitrary"),
        name="s5_ssm",
    )(u2, bt(b_re), bt(b_im), ct(c_re), ct(c_im), rowv(a_re), rowv(a_im), rowv(ldt),
      colv(a_re), colv(a_im), colv(ldt), d.reshape(g8, 1, gw))


def _sconv_kernel(gb_ref, gc_ref, xc_ref, w_ref, o_ref):
    p = gc_ref[...].astype(F32) * xc_ref[...].astype(F32)
    w = w_ref[...]
    row = lax.broadcasted_iota(jnp.int32, p.shape, 0)
    acc = p * w[C_CONV - 1:C_CONV, :]
    for s in range(1, C_CONV):
        acc = acc + _shift_rows(p, s, row) * w[C_CONV - 1 - s:C_CONV - s, :]
    o_ref[...] = (gb_ref[...].astype(F32) * acc).astype(o_ref.dtype)


def _sconv(proj, conv_w, bsz, seq):
    width = conv_w.shape[1]
    tc = 256
    nb = width // tc
    col = lambda off: pl.BlockSpec((seq, tc), lambda b, j, off=off: (b, off * nb + j))
    return pl.pallas_call(
        _sconv_kernel,
        grid=(bsz, nb),
        in_specs=[col(0), col(1), col(2), pl.BlockSpec((C_CONV, tc), lambda b, j: (0, j))],
        out_specs=pl.BlockSpec((seq, tc), lambda b, j: (b, j)),
        out_shape=jax.ShapeDtypeStruct((bsz * seq, width), BF16),
        compiler_params=_params("arbitrary", "arbitrary"),
        name="short_conv",
    )(proj, proj, proj, conv_w)


NEG_BIG = -1e30


def _dattn_kernel(q_ref, k_ref, vt_ref, lq1_ref, lk1_ref, lq2_ref, lk2_ref, ng_ref, o_ref,
                  acc_s, *, heads, tq, lambda_init):
    dh = DA_HEAD_DIM
    qi = pl.program_id(1)
    scale = dh ** -0.5
    lam = (jnp.exp(jnp.sum(lq1_ref[...] * lk1_ref[...], axis=-1, keepdims=True))
           - jnp.exp(jnp.sum(lq2_ref[...] * lk2_ref[...], axis=-1, keepdims=True)) + lambda_init)
    ki = lax.broadcasted_iota(jnp.int32, (tq, tq), 0)
    qj = lax.broadcasted_iota(jnp.int32, (tq, tq), 1)
    rel = (qj - ki).astype(F32)
    visible = ki <= qj
    ng = ng_ref[...]

    hpi = 2
    for h0 in range(0, heads, hpi):
        slopes = [2.0 ** (-8.0 * (h + 1) / heads) for h in range(h0, h0 + hpi)]

        def block(j, carry, masked, h0=h0, slopes=slopes):
            k0 = pl.multiple_of(j * tq, tq)
            off = ((qi - j) * tq).astype(F32)
            out = []
            for t in range(hpi):
                v_lo = (h0 + t) * 2 * dh
                vt = vt_ref[0, v_lo:v_lo + 2 * dh, pl.ds(k0, tq)]
                bias = -slopes[t] * (rel + off)
                for mp in range(2):
                    c = 2 * t + mp
                    m, l = carry[2 * c], carry[2 * c + 1]
                    lo = v_lo + mp * dh
                    s = _dot_nt(k_ref[pl.ds(k0, tq), lo:lo + dh], q_ref[:, lo:lo + dh]) * scale + bias
                    if masked:
                        s = jnp.where(visible, s, NEG_BIG)
                    m_new = jnp.maximum(m, jnp.max(s, axis=0, keepdims=True))
                    corr = jnp.exp(m - m_new)
                    p = jnp.exp(s - m_new)
                    out += [m_new, corr * l + jnp.sum(p, axis=0, keepdims=True)]
                    acc_s[c] = corr * acc_s[c] + _dot(vt, p.astype(BF16))
            return tuple(out)

        acc_s[...] = jnp.zeros(acc_s.shape, F32)
        neg = jnp.full((1, tq), NEG_BIG, F32)
        zero = jnp.zeros((1, tq), F32)
        carry = lax.fori_loop(0, qi, functools.partial(block, masked=False), (neg, zero) * (2 * hpi))
        carry = block(qi, carry, True)
        for t in range(hpi):
            v_lo = (h0 + t) * 2 * dh
            l1, l2 = carry[4 * t + 1], carry[4 * t + 3]
            o = acc_s[2 * t] / l1 - lam * (acc_s[2 * t + 1] / l2)
            o = o * lax.rsqrt(jnp.mean(o * o, axis=0, keepdims=True) + EPS)
            o_ref[:, v_lo:v_lo + 2 * dh] = (o.T * ng * (1.0 - lambda_init)).astype(o_ref.dtype)


def _dattn(proj, vt, lq1, lk1, lq2, lk2, norm_g, lambda_init, bsz, seq):
    dh = DA_HEAD_DIM
    width = vt.shape[1]
    heads = width // (2 * dh)
    tq = min(256, seq)
    nq = seq // tq
    vec = pl.BlockSpec((1, dh), lambda b, i: (0, 0))
    return pl.pallas_call(
        functools.partial(_dattn_kernel, heads=heads, tq=tq, lambda_init=lambda_init),
        grid=(bsz, nq),
        in_specs=[pl.BlockSpec((tq, width), lambda b, i: (b * nq + i, 3)),
                  pl.BlockSpec((seq, width), lambda b, i: (b, 4)),
                  pl.BlockSpec((1, width, seq), lambda b, i: (b, 0, 0)),
                  vec, vec, vec, vec, pl.BlockSpec((1, 2 * dh), lambda b, i: (0, 0))],
        out_specs=pl.BlockSpec((tq, width), lambda b, i: (b * nq + i, 0)),
        out_shape=jax.ShapeDtypeStruct((bsz * seq, width), BF16),
        scratch_shapes=[pltpu.VMEM((4, 2 * dh, tq), F32)],
        compiler_params=_params("arbitrary", "arbitrary"),
        name="diff_attention",
    )(proj, proj, vt, lq1.reshape(1, dh), lk1.reshape(1, dh), lq2.reshape(1, dh), lk2.reshape(1, dh),
      norm_g.reshape(1, 2 * dh))


def kernel(x, c, ada_w, ada_b, ln1_g, ln1_b, ln2_g, ln2_b, ffn_w_gate, ffn_w_up, ffn_w_down, ab_w_in, ab_w_out, dn_conv_w, dn_a_log, dn_dt_bias, dn_norm_g, s5_a_re, s5_a_im, s5_b_re, s5_b_im, s5_c_re, s5_c_im, s5_d, s5_log_dt, s5_w_glu, cd_w_in, cd_w_out, sc_conv_w, da_lq1, da_lk1, da_lq2, da_lk2, da_norm_g):
    bsz, seq, d = x.shape
    depth = ada_w.shape[0]
    m = bsz * seq
    a_width = dn_conv_w.shape[-1] // 3
    heads = dn_a_log.shape[-1]

    c_pad = jnp.zeros((8, d), F32).at[:bsz].set(c)
    mod = _ada_mod(c_pad, ada_w, ada_b)

    ab_main = ab_w_in[:, :, :4 * a_width]
    ab_out, cd_out, w_down = ab_w_out.astype(BF16), cd_w_out.astype(BF16), ffn_w_down.astype(BF16)
    x2 = x.reshape(m, d)
    h = _modulate(x2, mod, (0, 1), (0, 0), seq)
    for i in range(depth):
        j = i // 2
        if i % 2 == 0:
            w_uba = jnp.pad(jnp.concatenate([ab_w_in[j:j + 1, :, 4 * a_width + 2 * heads:],
                                             ab_w_in[j:j + 1, :, 4 * a_width:4 * a_width + 2 * heads]], axis=2),
                            ((0, 0), (0, 0), (0, LANES - 2 * heads)))
            b_width = w_uba.shape[2] - LANES
            proj = _mm(h, ab_main, j, 4 * a_width, BF16, 1024, 1024, "ab_in_proj")
            uf = _mm(h, w_uba, 0, w_uba.shape[2], F32, 1024, w_uba.shape[2], "ab_u_proj")
            ya = _deltanet(proj, uf[:, b_width:], dn_conv_w[j], dn_a_log[j], dn_dt_bias[j], dn_norm_g[j], bsz, seq)
            yb = _s5(uf, s5_a_re[j], s5_a_im[j], s5_b_re[j], s5_b_im[j],
                     s5_c_re[j], s5_c_im[j], s5_d[j], s5_log_dt[j], seq)
            yb = _glu(yb, s5_w_glu[j].astype(BF16), 1024)
            w_out = ab_out
        else:
            lambda_init = 0.8 - 0.6 * math.exp(-0.3 * i)
            c_width = sc_conv_w.shape[-1]
            proj = _mm(h, cd_w_in, j, 5 * c_width, BF16, 1024, 1024, "cd_in_proj")
            vt = _mm_t(h, cd_w_in[j, :, 5 * c_width:].T, bsz, seq, 512, 1024, "cd_value_proj")
            ya = _sconv(proj, sc_conv_w[j], bsz, seq)
            yb = _dattn(proj, vt, da_lq1[j], da_lk1[j], da_lq2[j], da_lk2[j], da_norm_g[j], lambda_init, bsz, seq)
            w_out = cd_out
        x2, h = _res_ln([ya, yb], w_out, j, x2, mod, (i, 2), ln1_g[i], ln1_b[i],
                        ((i, 4), (i, 3)), seq, 512, "mixer_out_ln")
        act = _gate_up(h, ffn_w_gate, ffn_w_up, i, 1024, 512)
        nxt = ((i + 1, 1), (i + 1, 0)) if i + 1 < depth else None
        x2, h = _res_ln([act], w_down, i, x2, mod, (i, 5), ln2_g[i], ln2_b[i],
                        nxt, seq, 256, "ffn_down_ln")
    return x2.reshape(bsz, seq, d)
```

```python
import functools
import math

import jax
import jax.numpy as jnp
from jax import lax
from jax.experimental import pallas as pl
from jax.experimental.pallas import tpu as pltpu

F32 = jnp.float32
BF16 = jnp.bfloat16

DEPTH = 4
A_HEAD_DIM = 128
A_CONV = 4
CHUNK = 64
S5_GROUP = 16
S5_STATE = 64
S5_STEP = 16
C_CONV = 3
DA_HEAD_DIM = 128
ALPHA = (2.0 * DEPTH) ** 0.25
EPS = 1e-5
LANES = 128
VMEM_LIMIT_BYTES = 56 * 1024 * 1024


def _params(*sem):
    return pltpu.CompilerParams(dimension_semantics=sem, vmem_limit_bytes=VMEM_LIMIT_BYTES)


def _silu(x):
    return x * jax.nn.sigmoid(x)


def _dot(a, b):
    return jnp.dot(a, b, preferred_element_type=F32)


def _dot_nt(a, b):
    return lax.dot_general(a, b, (((1,), (1,)), ((), ())), preferred_element_type=F32)


def _dot_tn(a, b):
    return lax.dot_general(a, b, (((0,), (0,)), ((), ())), preferred_element_type=F32)


def _split3(x):
    hi = x.astype(BF16)
    r1 = x - hi.astype(F32)
    mid = r1.astype(BF16)
    lo = (r1 - mid.astype(F32)).astype(BF16)
    return hi, mid, lo


def _dot_f32(a, b):
    a0, a1, a2 = _split3(a)
    b0, b1, b2 = _split3(b)
    return (_dot(a0, b0) + (_dot(a0, b1) + _dot(a1, b0))
            + (_dot(a1, b1) + _dot(a0, b2) + _dot(a2, b0)))


def _ada_kernel(c_ref, w_ref, b_ref, o_ref):
    c = c_ref[...]
    ca = _silu(c).astype(BF16)
    o_ref[0] = _dot(ca, w_ref[0].astype(BF16)) + b_ref[0]


def _ada_mod(c_pad, ada_w, ada_b):
    depth, d, n = ada_w.shape
    rows = c_pad.shape[0]
    tn = 1024
    return pl.pallas_call(
        _ada_kernel,
        grid=(depth, n // tn),
        in_specs=[pl.BlockSpec((rows, d), lambda i, j: (0, 0)),
                  pl.BlockSpec((1, d, tn), lambda i, j: (i, 0, j)),
                  pl.BlockSpec((1, 1, tn), lambda i, j: (i, 0, j))],
        out_specs=pl.BlockSpec((1, rows, tn), lambda i, j: (i, 0, j)),
        out_shape=jax.ShapeDtypeStruct((depth, rows, n), F32),
        compiler_params=_params("arbitrary", "arbitrary"),
        name="ada_mod",
    )(c_pad, ada_w, ada_b.reshape(depth, 1, n))


def _mod_spec(d, sel):
    layer, col = sel
    return pl.BlockSpec((1, 8, d), lambda *_: (layer, 0, col))


def _modulate_kernel(x_ref, sc_ref, sh_ref, o_ref, *, nl):
    b = pl.program_id(0) // nl
    o_ref[...] = (x_ref[...] * (1.0 + sc_ref[0, pl.ds(b, 1), :]) + sh_ref[0, pl.ds(b, 1), :]).astype(o_ref.dtype)


def _modulate(x2, mod, sc, sh, seq):
    m, d = x2.shape
    tl = min(seq, 512)
    return pl.pallas_call(
        functools.partial(_modulate_kernel, nl=seq // tl),
        grid=(m // tl,),
        in_specs=[pl.BlockSpec((tl, d), lambda i: (i, 0)), _mod_spec(d, sc), _mod_spec(d, sh)],
        out_specs=pl.BlockSpec((tl, d), lambda i: (i, 0)),
        out_shape=jax.ShapeDtypeStruct((m, d), BF16),
        compiler_params=_params("arbitrary"),
        name="modulate",
    )(x2, mod, mod)


def _mm_kernel(a_ref, w_ref, o_ref, w_s):
    @pl.when(pl.program_id(1) == 0)
    def _():
        w_s[...] = w_ref[0].astype(BF16)

    o_ref[...] = _dot(a_ref[...], w_s[...]).astype(o_ref.dtype)


def _mm(a, w, layer, n, out_dtype, tm, tn, name):
    m, k = a.shape
    tm = min(tm, m)
    tn = min(tn, n)
    return pl.pallas_call(
        _mm_kernel,
        grid=(n // tn, m // tm),
        in_specs=[pl.BlockSpec((tm, k), lambda j, i: (i, 0)),
                  pl.BlockSpec((1, k, tn), lambda j, i: (layer, 0, j))],
        out_specs=pl.BlockSpec((tm, tn), lambda j, i: (i, j)),
        out_shape=jax.ShapeDtypeStruct((m, n), out_dtype),
        scratch_shapes=[pltpu.VMEM((k, tn), BF16)],
        compiler_params=_params("arbitrary", "arbitrary"),
        name=name,
    )(a, w)


def _mm_t_kernel(wt_ref, a_ref, o_ref, w_s):
    @pl.when(pl.program_id(1) == 0)
    def _():
        w_s[...] = wt_ref[...].astype(BF16)

    o_ref[0] = _dot_nt(w_s[...], a_ref[...]).astype(o_ref.dtype)


def _mm_t(a, wt, bsz, seq, tn, tl, name):
    m, k = a.shape
    n = wt.shape[0]
    tl = min(tl, seq)
    nl = seq // tl
    return pl.pallas_call(
        _mm_t_kernel,
        grid=(n // tn, m // tl),
        in_specs=[pl.BlockSpec((tn, k), lambda j, i: (j, 0)),
                  pl.BlockSpec((tl, k), lambda j, i: (i, 0))],
        out_specs=pl.BlockSpec((1, tn, tl), lambda j, i: (i // nl, j, i % nl)),
        out_shape=jax.ShapeDtypeStruct((bsz, n, seq), BF16),
        scratch_shapes=[pltpu.VMEM((tn, k), BF16)],
        compiler_params=_params("arbitrary", "arbitrary"),
        name=name,
    )(wt, a)


def _gate_up_kernel(a_ref, wg_ref, wu_ref, o_ref, wg_s, wu_s):
    @pl.when(pl.program_id(1) == 0)
    def _():
        wg_s[...] = wg_ref[0].astype(BF16)
        wu_s[...] = wu_ref[0].astype(BF16)

    a = a_ref[...]
    g = _dot(a, wg_s[...])
    u = _dot(a, wu_s[...])
    o_ref[...] = (_silu(g) * u).astype(o_ref.dtype)


def _gate_up(a, wg, wu, layer, tm, tn):
    m, k = a.shape
    n = wg.shape[2]
    tm = min(tm, m)
    return pl.pallas_call(
        _gate_up_kernel,
        grid=(n // tn, m // tm),
        in_specs=[pl.BlockSpec((tm, k), lambda j, i: (i, 0)),
                  pl.BlockSpec((1, k, tn), lambda j, i: (layer, 0, j)),
                  pl.BlockSpec((1, k, tn), lambda j, i: (layer, 0, j))],
        out_specs=pl.BlockSpec((tm, tn), lambda j, i: (i, j)),
        out_shape=jax.ShapeDtypeStruct((m, n), BF16),
        scratch_shapes=[pltpu.VMEM((k, tn), BF16), pltpu.VMEM((k, tn), BF16)],
        compiler_params=_params("arbitrary", "arbitrary"),
        name="ffn_gate_up",
    )(a, wg, wu)


def _glu_kernel(y_ref, w_ref, o_ref):
    y = y_ref[...]
    t = _dot(y, w_ref[...])
    o_ref[...] = (y.astype(F32) * jax.nn.sigmoid(t)).astype(o_ref.dtype)


def _glu(y, w, tm):
    m, k = y.shape
    tm = min(tm, m)
    return pl.pallas_call(
        _glu_kernel,
        grid=(m // tm,),
        in_specs=[pl.BlockSpec((tm, k), lambda i: (i, 0)),
                  pl.BlockSpec((k, k), lambda i: (0, 0))],
        out_specs=pl.BlockSpec((tm, k), lambda i: (i, 0)),
        out_shape=jax.ShapeDtypeStruct((m, k), BF16),
        compiler_params=_params("arbitrary"),
        name="s5_glu",
    )(y, w)


def _res_ln_kernel(*refs, n_in, has_next, nl):
    a_refs = refs[:n_in]
    w_refs = refs[n_in:2 * n_in]
    x_ref, g_ref, lng_ref, lnb_ref = refs[2 * n_in:2 * n_in + 4]
    pos = 2 * n_in + 4
    if has_next:
        sc_ref, sh_ref = refs[pos:pos + 2]
        pos += 2
    xo_ref = refs[pos]
    pos += 1
    if has_next:
        ho_ref = refs[pos]
        pos += 1
    y_bufs = refs[pos:pos + 2]
    i = pl.program_id(0)
    b = jnp.maximum(i - 1, 0) // nl

    @pl.when(i == 0)
    def _():
        y_bufs[1][...] = jnp.zeros(y_bufs[1].shape, F32)

    def step(y_new, y_old):
        part = _dot(a_refs[0][...], w_refs[0][0])
        for a_ref, w_ref in zip(a_refs[1:], w_refs[1:]):
            part = part + _dot(a_ref[...], w_ref[0])
        y_new[...] = part
        r = ALPHA * x_ref[...] + (1.0 + g_ref[0, pl.ds(b, 1), :]) * y_old[...]
        mu = jnp.mean(r, axis=-1, keepdims=True)
        rc = r - mu
        var = jnp.mean(rc * rc, axis=-1, keepdims=True)
        xn = rc * lax.rsqrt(var + EPS) * lng_ref[...] + lnb_ref[...]
        xo_ref[...] = xn
        if has_next:
            ho_ref[...] = (xn * (1.0 + sc_ref[0, pl.ds(b, 1), :]) + sh_ref[0, pl.ds(b, 1), :]).astype(ho_ref.dtype)

    even = lax.rem(i, 2) == 0

    @pl.when(even)
    def _():
        step(y_bufs[0], y_bufs[1])

    @pl.when(jnp.logical_not(even))
    def _():
        step(y_bufs[1], y_bufs[0])


def _res_ln(a_list, w, layer, x2, mod, gate, ln_g, ln_b, nxt, seq, tm, name):
    m, d = x2.shape
    n_in = len(a_list)
    ka = a_list[0].shape[1]
    tm = min(tm, m, seq)
    nt = m // tm
    nl = seq // tm
    has_next = nxt is not None
    cur = lambda i: (jnp.minimum(i, nt - 1), 0)
    prev = lambda i: (jnp.maximum(i - 1, 0), 0)
    in_specs = [pl.BlockSpec((tm, ka), cur) for _ in a_list]
    in_specs += [pl.BlockSpec((1, ka, d), functools.partial(lambda i, s: (layer, s, 0), s=s),
                              pipeline_mode=pl.Buffered(1)) for s in range(n_in)]
    pvec = pl.BlockSpec((1, d), lambda i: (0, 0))
    in_specs += [pl.BlockSpec((tm, d), prev), _mod_spec(d, gate), pvec, pvec]
    args = list(a_list) + [w] * n_in + [x2, mod, ln_g.reshape(1, d), ln_b.reshape(1, d)]
    out_shape = [jax.ShapeDtypeStruct((m, d), F32)]
    out_specs = [pl.BlockSpec((tm, d), prev)]
    if has_next:
        in_specs += [_mod_spec(d, nxt[0]), _mod_spec(d, nxt[1])]
        args += [mod, mod]
        out_shape.append(jax.ShapeDtypeStruct((m, d), BF16))
        out_specs.append(pl.BlockSpec((tm, d), prev))
    out = pl.pallas_call(
        functools.partial(_res_ln_kernel, n_in=n_in, has_next=has_next, nl=nl),
        grid=(nt + 1,),
        in_specs=in_specs,
        out_specs=out_specs,
        out_shape=out_shape,
        scratch_shapes=[pltpu.VMEM((tm, d), F32), pltpu.VMEM((tm, d), F32)],
        compiler_params=_params("arbitrary"),
        name=name,
    )(*args)
    return (out[0], out[1]) if has_next else (out[0], None)


def _shift_rows(x, s, row):
    return jnp.where(row >= s, pltpu.roll(x, s, axis=0), 0.0)


def _conv_silu(x_ref, cw_ref):
    x = x_ref[...].astype(F32)
    w = cw_ref[...]
    row = lax.broadcasted_iota(jnp.int32, x.shape, 0)
    acc = x * w[A_CONV - 1:A_CONV, :]
    for s in range(1, A_CONV):
        acc = acc + _shift_rows(x, s, row) * w[A_CONV - 1 - s:A_CONV - s, :]
    return _silu(acc)


def _l2n(t):
    return t * lax.rsqrt(jnp.sum(t * t, axis=-1, keepdims=True) + 1e-6)


DN_HB = 4


def _delta_kernel(q_ref, k_ref, v_ref, z_ref, cwq_ref, cwk_ref, cwv_ref, a_ref, b_ref,
                  alog_ref, dtb_ref, ng_ref, o_ref,
                  q_s, k_s, v_s, gc_s, bt_s, u_s, wq_s, l2_s, gl_s, s_s, n_s, nt_s, x_s, *, nc, seq):
    c_len, dh = CHUNK, A_HEAD_DIM
    nmat_all = DN_HB * nc
    hh = lax.rem(pl.program_id(1), DN_HB)
    q_s[hh] = _l2n(_conv_silu(q_ref, cwq_ref)) * (dh ** -0.5)
    k_s[hh] = _l2n(_conv_silu(k_ref, cwk_ref))
    v_s[hh] = _conv_silu(v_ref, cwv_ref)

    a = a_ref[0, 0]
    b = b_ref[0, 0]
    sp_in = a + dtb_ref[0]
    softplus = jnp.maximum(sp_in, 0.0) + jnp.log1p(jnp.exp(-jnp.abs(sp_in)))
    g = -jnp.exp(alog_ref[0]) * softplus
    ri = lax.broadcasted_iota(jnp.int32, (c_len, LANES), 0)
    ci = lax.broadcasted_iota(jnp.int32, (c_len, LANES), 1)
    tril, strict, eye = ri >= ci, ri > ci, ri == ci
    eye64 = eye[:, :c_len]
    eye_bf = jnp.where(eye64, 1.0, 0.0).astype(BF16)
    g0, g1, g2 = _split3(g)
    triu = jnp.where(ci < c_len, jnp.where(ri <= ci, 1.0, 0.0), 0.0).astype(BF16)
    gc_s[hh] = _dot(g0, triu) + _dot(g1, triu) + _dot(g2, triu)
    bt_s[hh] = jax.nn.sigmoid(b)

    def prep(c, carry):
        r0 = pl.multiple_of(c * c_len, c_len)
        for hd in range(DN_HB):
            kc = k_s[hd, pl.ds(r0, c_len), :]
            qc = q_s[hd, pl.ds(r0, c_len), :]
            grow = gc_s[hd, pl.ds(c, 1), :]
            brow = bt_s[hd, pl.ds(c, 1), :]
            gcol = jnp.sum(jnp.where(eye, grow, 0.0), axis=1, keepdims=True)
            bcol = jnp.sum(jnp.where(eye64, brow, 0.0), axis=1, keepdims=True)
            decay = jnp.where(tril, jnp.exp(jnp.where(tril, gcol - grow, 0.0)), 0.0)
            kb = kc.astype(BF16)
            kk = _dot_nt(kb, jnp.concatenate([kb, jnp.zeros_like(kb)], axis=0))
            m0 = pl.multiple_of((hd * nc + c) * c_len, c_len)
            n_s[pl.ds(m0, c_len), :] = jnp.where(strict, kk * bcol * decay, 0.0)
            qk = _dot_nt(qc.astype(BF16), kb)
            glast = grow[:, c_len - 1:c_len]
            kt = (kc * jnp.exp(glast - gcol)).astype(BF16)
            wq_s[hd * nc + c, c_len:, :] = (qc * jnp.exp(gcol)).astype(BF16)
            l2_s[hd * nc + c, :c_len, :] = (qk * decay[:, :c_len]).astype(BF16)
            l2_s[hd * nc + c, c_len:, :] = _dot_tn(kt, eye_bf).astype(BF16)
            gl_s[hd, pl.ds(c, 1), :] = jnp.broadcast_to(jnp.exp(glast), (1, LANES))
        return carry

    @pl.when(hh == DN_HB - 1)
    def _():
        lax.fori_loop(0, nc, prep, 0)
        x_s[...] = jnp.zeros(x_s.shape, F32)
        for i in range(c_len):
            slab = n_s[pl.ds(i, nmat_all, stride=c_len), :]
            nt_s[i] = slab.T[:c_len, :]
        sub = lax.broadcasted_iota(jnp.int32, (8, LANES), 0)
        for ib in range(c_len // 8):
            npc = ib + 1

            def row(ii, carry, ib=ib, npc=npc):
                i = ib * 8 + ii
                parts = [[None] * 4 for _ in range(npc)]
                for j in range(8 * npc):
                    coef = nt_s[i, pl.ds(j, 1), :]
                    for p in range(j // 8 + 1):
                        term = coef * x_s[j, p * 8:(p + 1) * 8, :]
                        k = j % 4
                        parts[p][k] = term if parts[p][k] is None else parts[p][k] + term
                for p in range(npc):
                    live = [t for t in parts[p] if t is not None]
                    tot = live[0]
                    for t in live[1:]:
                        tot = tot + t
                    x_s[i, p * 8:(p + 1) * 8, :] = jnp.where(sub + p * 8 == i, 1.0, 0.0) - tot
                return carry

            lax.fori_loop(0, 8, row, 0)
        zpad = jnp.zeros((LANES - c_len, LANES), F32)
        for i in range(c_len):
            n_s[pl.ds(i, nmat_all, stride=c_len), :] = jnp.concatenate([x_s[i], zpad], axis=0).T

        def uw(hd, c):
            r0 = pl.multiple_of(c * c_len, c_len)
            idx = hd * nc + c
            t = n_s[pl.ds(pl.multiple_of(idx * c_len, c_len), c_len), :][:, :c_len]
            tb = t * bt_s[hd, pl.ds(c, 1), :]
            tw = tb * jnp.exp(gc_s[hd, pl.ds(c, 1), :][:, :c_len])
            kb = k_s[hd, pl.ds(r0, c_len), :].astype(BF16)
            vb = v_s[hd, pl.ds(r0, c_len), :].astype(BF16)
            u_s[hd, pl.ds(r0, c_len), :] = _dot(tb.astype(BF16), vb)
            wq_s[idx, :c_len, :] = _dot(tw.astype(BF16), kb).astype(BF16)

        for hd in range(DN_HB):
            uw(hd, 0)
        ng = ng_ref[...]
        s_s[...] = jnp.zeros(s_s.shape, F32)

        def scan(c, carry):
            r0 = pl.multiple_of(c * c_len, c_len)
            for hd in range(DN_HB):
                s = s_s[hd]
                sb = s.astype(BF16)
                r1 = _dot(wq_s[hd * nc + c], sb)
                v_new = u_s[hd, pl.ds(r0, c_len), :] - r1[:c_len]
                r2 = _dot(l2_s[hd * nc + c], v_new.astype(BF16))
                o = r1[c_len:] + r2[:c_len]
                s_s[hd] = s * gl_s[hd, pl.ds(c, 1), :] + r2[c_len:]
                q_s[hd, pl.ds(r0, c_len), :] = o
            cn = jnp.minimum(c + 1, nc - 1)
            for hd in range(DN_HB):
                uw(hd, cn)
            return carry

        lax.fori_loop(0, nc, scan, 0)

        rt = 256

        def gate(r, carry):
            r0 = pl.multiple_of(r * rt, rt)
            for hd in range(DN_HB):
                zc = z_ref[pl.ds(r0, rt), hd * dh:(hd + 1) * dh].astype(F32)
                o = q_s[hd, pl.ds(r0, rt), :]
                on = o * lax.rsqrt(jnp.mean(o * o, axis=-1, keepdims=True) + EPS) * ng
                o_ref[pl.ds(r0, rt), hd * dh:(hd + 1) * dh] = (on * _silu(zc)).astype(o_ref.dtype)
            return carry

        lax.fori_loop(0, seq // rt, gate, 0)


def _deltanet(proj, ba, conv_w, a_log, dt_bias, norm_g, bsz, seq):
    dh = A_HEAD_DIM
    heads = a_log.shape[0]
    nc = seq // CHUNK
    assert DN_HB * nc == LANES and heads % DN_HB == 0
    bh = ba[:, :2 * heads].reshape(bsz, seq, 2, heads).transpose(2, 0, 3, 1).reshape(2, bsz, heads, nc, CHUNK)
    col = lambda off: pl.BlockSpec((seq, dh), lambda b, h, off=off: (b, off + h))
    cw = lambda off: pl.BlockSpec((A_CONV, dh), lambda b, h, off=off: (0, off + h))
    rows = pl.BlockSpec((1, 1, nc, CHUNK), lambda b, h: (b, h, 0, 0))
    scal = pl.BlockSpec((1, 1, 1), lambda b, h: (h, 0, 0))
    hgroups = heads // DN_HB
    return pl.pallas_call(
        functools.partial(_delta_kernel, nc=nc, seq=seq),
        grid=(bsz, heads),
        in_specs=[col(0), col(heads), col(2 * heads),
                  pl.BlockSpec((seq, DN_HB * dh), lambda b, h: (b, 3 * hgroups + h // DN_HB)),
                  cw(0), cw(heads), cw(2 * heads), rows, rows, scal, scal,
                  pl.BlockSpec((1, dh), lambda b, h: (0, 0))],
        out_specs=pl.BlockSpec((seq, DN_HB * dh), lambda b, h: (b, h // DN_HB)),
        out_shape=jax.ShapeDtypeStruct((bsz * seq, heads * dh), BF16),
        scratch_shapes=[pltpu.VMEM((DN_HB, seq, dh), F32), pltpu.VMEM((DN_HB, seq, dh), F32),
                        pltpu.VMEM((DN_HB, seq, dh), F32),
                        pltpu.VMEM((DN_HB, nc, LANES), F32), pltpu.VMEM((DN_HB, nc, CHUNK), F32),
                        pltpu.VMEM((DN_HB, seq, dh), F32),
                        pltpu.VMEM((DN_HB * nc, 2 * CHUNK, dh), BF16), pltpu.VMEM((DN_HB * nc, CHUNK + dh, CHUNK), BF16),
                        pltpu.VMEM((DN_HB, nc, LANES), F32), pltpu.VMEM((DN_HB, dh, dh), F32),
                        pltpu.VMEM((DN_HB * nc * CHUNK, LANES), F32), pltpu.VMEM((CHUNK, CHUNK, LANES), F32),
                        pltpu.VMEM((CHUNK, CHUNK, LANES), F32)],
        compiler_params=_params("arbitrary", "arbitrary"),
        name="gated_deltanet",
    )(proj, proj, proj, proj, conv_w, conv_w, conv_w, bh[1], bh[0],
      a_log.reshape(heads, 1, 1), dt_bias.reshape(heads, 1, 1), norm_g.reshape(1, dh))


def _cexp(re, im):
    m = jnp.exp(re)
    return m * jnp.cos(im), m * jnp.sin(im)


def _cmul(ar, ai, br, bi):
    return ar * br - ai * bi, ar * bi + ai * br


def _gelu_tanh(x):
    return 0.5 * x * (1.0 + jnp.tanh(math.sqrt(2.0 / math.pi) * (x + 0.044715 * (x * x * x))))


S5_GB = LANES // S5_GROUP


def _s5_kernel(u_ref, bre_ref, bim_ref, cre_ref, cim_ref, lr_r_ref, li_r_ref, dt_r_ref,
               lr_c_ref, li_c_ref, dt_c_ref, d_ref, o_ref, ua_s, bps_s, cq_s, ks_s, y_s, *, nc):
    step, grp, st = S5_STEP, S5_GROUP, S5_STATE
    gw = S5_GB * grp
    half = S5_GB * st
    sw = 2 * half
    r = ua_s.shape[0]
    gsh, ssh = grp.bit_length() - 1, st.bit_length() - 1

    lr, li = lr_r_ref[0], li_r_ref[0]
    dt = jnp.exp(dt_r_ref[0])
    ar, ai = _cexp(lr * dt, li * dt)
    den = lr * lr + li * li
    pr = ((ar - 1.0) * lr + ai * li) / den
    pi_ = (ai * lr - (ar - 1.0) * li) / den
    re_lane = lax.broadcasted_iota(jnp.int32, (1, sw), 1) < half
    row_g = lax.shift_right_logical(lax.broadcasted_iota(jnp.int32, (gw, sw), 0), gsh)
    lane_g = lax.shift_right_logical(lax.broadcasted_iota(jnp.int32, (gw, sw), 1) & (half - 1), ssh)
    same_g = row_g == lane_g
    br = jnp.where(same_g, jnp.concatenate([bre_ref[0]] * (sw // LANES), axis=1), 0.0)
    bi = jnp.where(same_g, jnp.concatenate([bim_ref[0]] * (sw // LANES), axis=1), 0.0)
    bp0 = None
    for k in range(step):
        blk = br * jnp.where(re_lane, pr, pi_) + bi * jnp.where(re_lane, -pi_, pr)
        if k == 0:
            bp0 = blk
        s = step - 1 - k
        bps_s[s * gw:(s + 1) * gw, :] = blk.astype(BF16)
        pr, pi_ = _cmul(pr, pi_, ar, ai)

    dtc = jnp.exp(dt_c_ref[0])
    acr_d, aci_d = _cexp(lr_c_ref[0] * dtc, li_c_ref[0] * dtc)
    hr = half // LANES

    def spread(v):
        vt = jnp.concatenate([v, jnp.zeros((LANES - hr, LANES), F32)], axis=0).T
        return jnp.concatenate([jnp.broadcast_to(vt[:, q:q + 1], (LANES, gw)) for q in range(hr)], axis=0)

    acr, aci = spread(acr_d), spread(aci_d)
    same_c = (lax.shift_right_logical(lax.broadcasted_iota(jnp.int32, (half, gw), 0), ssh)
              == lax.shift_right_logical(lax.broadcasted_iota(jnp.int32, (half, gw), 1), gsh))
    pcr = jnp.where(same_c, cre_ref[0], 0.0)
    pci = jnp.where(same_c, cim_ref[0], 0.0)
    bp_hi = bp0.astype(BF16)
    bp_lo = (bp0 - bp_hi.astype(F32)).astype(BF16)
    for k in range(step + 1):
        re_hi, im_hi = pcr.astype(BF16), (-pci).astype(BF16)
        cq_hi = jnp.concatenate([re_hi, im_hi], axis=0)
        if k < step:
            cq_lo = jnp.concatenate([(pcr - re_hi.astype(F32)).astype(BF16),
                                     (-pci - im_hi.astype(F32)).astype(BF16)], axis=0)
            taps = _dot(bp_hi, cq_hi) + (_dot(bp_hi, cq_lo) + _dot(bp_lo, cq_hi))
            ks_s[(step - 1 - k) * gw:(step - k) * gw, :] = taps.astype(BF16)
            pcr, pci = _cmul(pcr, pci, acr, aci)
        if k >= 1:
            cq_s[k - 1] = cq_hi

    for s in range(step):
        ua_s[:, s * gw:(s + 1) * gw] = u_ref[pl.ds(s, r, stride=step), :].astype(BF16)
    x = _dot(ua_s[...], bps_s[...])
    blk_i = lax.broadcasted_iota(jnp.int32, (r, sw), 0) & (nc - 1)
    er, ei = ar, ai
    for _ in range(step.bit_length() - 1):
        er, ei = _cmul(er, ei, er, ei)
    sh = 1
    while sh < nc:
        xs = jnp.where(blk_i >= sh, pltpu.roll(x, sh, axis=0), 0.0)
        x = x + xs * er + pltpu.roll(xs, half, axis=1) * jnp.where(re_lane, -ei, ei)
        er, ei = _cmul(er, ei, er, ei)
        sh *= 2
    xp = jnp.where(blk_i >= 1, pltpu.roll(x, 1, axis=0), 0.0).astype(BF16)

    d = d_ref[0]
    for t in range(step):
        y = (_dot(ua_s[:, :(t + 1) * gw], ks_s[(step - 1 - t) * gw:, :]) + _dot(xp, cq_s[t])
             + d * u_ref[pl.ds(t, r, stride=step), :])
        y_s[pl.ds(t, r, stride=step), :] = _gelu_tanh(y)

    rt = 512

    def emit(i, carry):
        r0 = pl.multiple_of(i * rt, rt)
        o_ref[pl.ds(r0, rt), :] = y_s[pl.ds(r0, rt), :].astype(o_ref.dtype)
        return carry

    lax.fori_loop(0, y_s.shape[0] // rt, emit, 0)


def _s5(u2, a_re, a_im, b_re, b_im, c_re, c_im, d, log_dt, seq):
    groups, st = a_re.shape
    grp, step = S5_GROUP, S5_STEP
    m = u2.shape[0]
    g8 = groups // S5_GB
    gw, half = S5_GB * grp, S5_GB * st
    sw = 2 * half
    nc = seq // step
    assert nc & (nc - 1) == 0 and gw == LANES
    r = m // step
    bt = lambda t: jnp.tile(t.transpose(0, 2, 1).reshape(g8, gw, st), (1, 1, 2))
    ct = lambda t: jnp.tile(t.transpose(0, 2, 1).reshape(g8, half, grp), (1, 1, S5_GB))
    rowv = lambda t: jnp.tile(t.reshape(g8, 1, half), (1, 1, 2))
    colv = lambda t: t.reshape(g8, half // LANES, LANES)
    ldt = jnp.broadcast_to(log_dt[:, None], (groups, st))
    g3 = lambda shp: pl.BlockSpec((1,) + shp, lambda g: (g, 0, 0))
    dense = g3((half // LANES, LANES))
    return pl.pallas_call(
        functools.partial(_s5_kernel, nc=nc),
        grid=(g8,),
        in_specs=[pl.BlockSpec((m, gw), lambda g: (0, g)),
                  g3((gw, 2 * st)), g3((gw, 2 * st)), g3((half, gw)), g3((half, gw)),
                  g3((1, sw)), g3((1, sw)), g3((1, sw)), dense, dense, dense,
                  g3((1, gw))],
        out_specs=pl.BlockSpec((m, gw), lambda g: (0, g)),
        out_shape=jax.ShapeDtypeStruct((m, groups * grp), BF16),
        scratch_shapes=[pltpu.VMEM((r, step * gw), BF16), pltpu.VMEM((step * gw, sw), BF16),
                        pltpu.VMEM((step, sw, gw), BF16), pltpu.VMEM((step * gw, gw), BF16),
                        pltpu.VMEM((m, gw), F32)],
        compiler_params=_params("arbitrary"),
        name="s5_ssm",
    )(u2, bt(b_re), bt(b_im), ct(c_re), ct(c_im), rowv(a_re), rowv(a_im), rowv(ldt),
      colv(a_re), colv(a_im), colv(ldt), d.reshape(g8, 1, gw))


def _sconv_kernel(gb_ref, gc_ref, xc_ref, w_ref, o_ref):
    p = gc_ref[...].astype(F32) * xc_ref[...].astype(F32)
    w = w_ref[...]
    row = lax.broadcasted_iota(jnp.int32, p.shape, 0)
    acc = p * w[C_CONV - 1:C_CONV, :]
    for s in range(1, C_CONV):
        acc = acc + _shift_rows(p, s, row) * w[C_CONV - 1 - s:C_CONV - s, :]
    o_ref[...] = (gb_ref[...].astype(F32) * acc).astype(o_ref.dtype)


def _sconv(proj, conv_w, bsz, seq):
    width = conv_w.shape[1]
    tc = 256
    nb = width // tc
    col = lambda off: pl.BlockSpec((seq, tc), lambda b, j, off=off: (b, off * nb + j))
    return pl.pallas_call(
        _sconv_kernel,
        grid=(bsz, nb),
        in_specs=[col(0), col(1), col(2), pl.BlockSpec((C_CONV, tc), lambda b, j: (0, j))],
        out_specs=pl.BlockSpec((seq, tc), lambda b, j: (b, j)),
        out_shape=jax.ShapeDtypeStruct((bsz * seq, width), BF16),
        compiler_params=_params("arbitrary", "arbitrary"),
        name="short_conv",
    )(proj, proj, proj, conv_w)


NEG_BIG = -1e30


def _dattn_kernel(q_ref, k_ref, vt_ref, lq1_ref, lk1_ref, lq2_ref, lk2_ref, ng_ref, o_ref,
                  acc_s, *, heads, tq, lambda_init):
    dh = DA_HEAD_DIM
    qi = pl.program_id(1)
    scale = dh ** -0.5
    lam = (jnp.exp(jnp.sum(lq1_ref[...] * lk1_ref[...], axis=-1, keepdims=True))
           - jnp.exp(jnp.sum(lq2_ref[...] * lk2_ref[...], axis=-1, keepdims=True)) + lambda_init)
    ki = lax.broadcasted_iota(jnp.int32, (tq, tq), 0)
    qj = lax.broadcasted_iota(jnp.int32, (tq, tq), 1)
    rel = (qj - ki).astype(F32)
    visible = ki <= qj
    ng = ng_ref[...]

    hpi = 2
    for h0 in range(0, heads, hpi):
        slopes = [2.0 ** (-8.0 * (h + 1) / heads) for h in range(h0, h0 + hpi)]

        def block(j, carry, masked, h0=h0, slopes=slopes):
            k0 = pl.multiple_of(j * tq, tq)
            off = ((qi - j) * tq).astype(F32)
            out = []
            for t in range(hpi):
                v_lo = (h0 + t) * 2 * dh
                vt = vt_ref[0, v_lo:v_lo + 2 * dh, pl.ds(k0, tq)]
                bias = -slopes[t] * (rel + off)
                for mp in range(2):
                    c = 2 * t + mp
                    m, l = carry[2 * c], carry[2 * c + 1]
                    lo = v_lo + mp * dh
                    s = _dot_nt(k_ref[pl.ds(k0, tq), lo:lo + dh], q_ref[:, lo:lo + dh]) * scale + bias
                    if masked:
                        s = jnp.where(visible, s, NEG_BIG)
                    m_new = jnp.maximum(m, jnp.max(s, axis=0, keepdims=True))
                    corr = jnp.exp(m - m_new)
                    p = jnp.exp(s - m_new)
                    out += [m_new, corr * l + jnp.sum(p, axis=0, keepdims=True)]
                    acc_s[c] = corr * acc_s[c] + _dot(vt, p.astype(BF16))
            return tuple(out)

        acc_s[...] = jnp.zeros(acc_s.shape, F32)
        neg = jnp.full((1, tq), NEG_BIG, F32)
        zero = jnp.zeros((1, tq), F32)
        carry = lax.fori_loop(0, qi, functools.partial(block, masked=False), (neg, zero) * (2 * hpi))
        carry = block(qi, carry, True)
        for t in range(hpi):
            v_lo = (h0 + t) * 2 * dh
            l1, l2 = carry[4 * t + 1], carry[4 * t + 3]
            o = acc_s[2 * t] / l1 - lam * (acc_s[2 * t + 1] / l2)
            o = o * lax.rsqrt(jnp.mean(o * o, axis=0, keepdims=True) + EPS)
            o_ref[:, v_lo:v_lo + 2 * dh] = (o.T * ng * (1.0 - lambda_init)).astype(o_ref.dtype)


def _dattn(proj, vt, lq1, lk1, lq2, lk2, norm_g, lambda_init, bsz, seq):
    dh = DA_HEAD_DIM
    width = vt.shape[1]
    heads = width // (2 * dh)
    tq = min(256, seq)
    nq = seq // tq
    vec = pl.BlockSpec((1, dh), lambda b, i: (0, 0))
    return pl.pallas_call(
        functools.partial(_dattn_kernel, heads=heads, tq=tq, lambda_init=lambda_init),
        grid=(bsz, nq),
        in_specs=[pl.BlockSpec((tq, width), lambda b, i: (b * nq + i, 3)),
                  pl.BlockSpec((seq, width), lambda b, i: (b, 4)),
                  pl.BlockSpec((1, width, seq), lambda b, i: (b, 0, 0)),
                  vec, vec, vec, vec, pl.BlockSpec((1, 2 * dh), lambda b, i: (0, 0))],
        out_specs=pl.BlockSpec((tq, width), lambda b, i: (b * nq + i, 0)),
        out_shape=jax.ShapeDtypeStruct((bsz * seq, width), BF16),
        scratch_shapes=[pltpu.VMEM((4, 2 * dh, tq), F32)],
        compiler_params=_params("arbitrary", "arbitrary"),
        name="diff_attention",
    )(proj, proj, vt, lq1.reshape(1, dh), lk1.reshape(1, dh), lq2.reshape(1, dh), lk2.reshape(1, dh),
      norm_g.reshape(1, 2 * dh))


def kernel(x, c, ada_w, ada_b, ln1_g, ln1_b, ln2_g, ln2_b, ffn_w_gate, ffn_w_up, ffn_w_down, ab_w_in, ab_w_out, dn_conv_w, dn_a_log, dn_dt_bias, dn_norm_g, s5_a_re, s5_a_im, s5_b_re, s5_b_im, s5_c_re, s5_c_im, s5_d, s5_log_dt, s5_w_glu, cd_w_in, cd_w_out, sc_conv_w, da_lq1, da_lk1, da_lq2, da_lk2, da_norm_g):
    bsz, seq, d = x.shape
    depth = ada_w.shape[0]
    m = bsz * seq
    a_width = dn_conv_w.shape[-1] // 3
    heads = dn_a_log.shape[-1]

    c_pad = jnp.zeros((8, d), F32).at[:bsz].set(c)
    mod = _ada_mod(c_pad, ada_w, ada_b)

    ab_out, cd_out, w_down = ab_w_out.astype(BF16), cd_w_out.astype(BF16), ffn_w_down.astype(BF16)
    x2 = x.reshape(m, d)
    h = _modulate(x2, mod, (0, 1), (0, 0), seq)
    for i in range(depth):
        j = i // 2
        if i % 2 == 0:
            w_uba = jnp.pad(jnp.concatenate([ab_w_in[j:j + 1, :, 4 * a_width + 2 * heads:],
                                             ab_w_in[j:j + 1, :, 4 * a_width:4 * a_width + 2 * heads]], axis=2),
                            ((0, 0), (0, 0), (0, LANES - 2 * heads)))
            b_width = w_uba.shape[2] - LANES
            proj = _mm(h, ab_w_in, j, 4 * a_width, BF16, 1024, 1024, "ab_in_proj")
            uf = _mm(h, w_uba, 0, w_uba.shape[2], F32, 1024, w_uba.shape[2], "ab_u_proj")
            ya = _deltanet(proj, uf[:, b_width:], dn_conv_w[j], dn_a_log[j], dn_dt_bias[j], dn_norm_g[j], bsz, seq)
            yb = _s5(uf, s5_a_re[j], s5_a_im[j], s5_b_re[j], s5_b_im[j],
                     s5_c_re[j], s5_c_im[j], s5_d[j], s5_log_dt[j], seq)
            yb = _glu(yb, s5_w_glu[j].astype(BF16), 1024)
            w_out = ab_out
        else:
            lambda_init = 0.8 - 0.6 * math.exp(-0.3 * i)
            c_width = sc_conv_w.shape[-1]
            proj = _mm(h, cd_w_in, j, 5 * c_width, BF16, 1024, 1024, "cd_in_proj")
            vt = _mm_t(h, cd_w_in[j, :, 5 * c_width:].T, bsz, seq, 512, 1024, "cd_value_proj")
            ya = _sconv(proj, sc_conv_w[j], bsz, seq)
            yb = _dattn(proj, vt, da_lq1[j], da_lk1[j], da_lq2[j], da_lk2[j], da_norm_g[j], lambda_init, bsz, seq)
            w_out = cd_out
        x2, h = _res_ln([ya, yb], w_out, j, x2, mod, (i, 2), ln1_g[i], ln1_b[i],
                        ((i, 4), (i, 3)), seq, 512, "mixer_out_ln")
        act = _gate_up(h, ffn_w_gate, ffn_w_up, i, 1024, 512)
        nxt = ((i + 1, 1), (i + 1, 0)) if i + 1 < depth else None
        x2, h = _res_ln([act], w_down, i, x2, mod, (i, 5), ln2_g[i], ln2_b[i],
                        nxt, seq, 256, "ffn_down_ln")
    return x2.reshape(bsz, seq, d)
```

```python
import functools
import math

import jax
import jax.numpy as jnp
from jax import lax
from jax.experimental import pallas as pl
from jax.experimental.pallas import tpu as pltpu

F32 = jnp.float32
BF16 = jnp.bfloat16

DEPTH = 4
A_HEAD_DIM = 128
A_CONV = 4
CHUNK = 64
S5_GROUP = 16
S5_STATE = 64
S5_STEP = 16
C_CONV = 3
DA_HEAD_DIM = 128
ALPHA = (2.0 * DEPTH) ** 0.25
EPS = 1e-5
LANES = 128
VMEM_LIMIT_BYTES = 56 * 1024 * 1024


def _params(*sem):
    return pltpu.CompilerParams(dimension_semantics=sem, vmem_limit_bytes=VMEM_LIMIT_BYTES)


def _silu(x):
    return x * jax.nn.sigmoid(x)


def _dot(a, b):
    return jnp.dot(a, b, preferred_element_type=F32)


def _dot_nt(a, b):
    return lax.dot_general(a, b, (((1,), (1,)), ((), ())), preferred_element_type=F32)


def _dot_tn(a, b):
    return lax.dot_general(a, b, (((0,), (0,)), ((), ())), preferred_element_type=F32)


def _split3(x):
    hi = x.astype(BF16)
    r1 = x - hi.astype(F32)
    mid = r1.astype(BF16)
    lo = (r1 - mid.astype(F32)).astype(BF16)
    return hi, mid, lo


def _dot_f32(a, b):
    a0, a1, a2 = _split3(a)
    b0, b1, b2 = _split3(b)
    return (_dot(a0, b0) + (_dot(a0, b1) + _dot(a1, b0))
            + (_dot(a1, b1) + _dot(a0, b2) + _dot(a2, b0)))


def _ada_kernel(c_ref, w_ref, b_ref, o_ref):
    c = c_ref[...]
    ca = _silu(c).astype(BF16)
    o_ref[0] = _dot(ca, w_ref[0].astype(BF16)) + b_ref[0]


def _ada_mod(c_pad, ada_w, ada_b):
    depth, d, n = ada_w.shape
    rows = c_pad.shape[0]
    tn = 1024
    return pl.pallas_call(
        _ada_kernel,
        grid=(depth, n // tn),
        in_specs=[pl.BlockSpec((rows, d), lambda i, j: (0, 0)),
                  pl.BlockSpec((1, d, tn), lambda i, j: (i, 0, j)),
                  pl.BlockSpec((1, 1, tn), lambda i, j: (i, 0, j))],
        out_specs=pl.BlockSpec((1, rows, tn), lambda i, j: (i, 0, j)),
        out_shape=jax.ShapeDtypeStruct((depth, rows, n), F32),
        compiler_params=_params("arbitrary", "arbitrary"),
        name="ada_mod",
    )(c_pad, ada_w, ada_b.reshape(depth, 1, n))


def _mod_spec(d, sel):
    layer, col = sel
    return pl.BlockSpec((1, 8, d), lambda *_: (layer, 0, col))


def _modulate_kernel(x_ref, sc_ref, sh_ref, o_ref, *, nl):
    b = pl.program_id(0) // nl
    o_ref[...] = (x_ref[...] * (1.0 + sc_ref[0, pl.ds(b, 1), :]) + sh_ref[0, pl.ds(b, 1), :]).astype(o_ref.dtype)


def _modulate(x2, mod, sc, sh, seq):
    m, d = x2.shape
    tl = min(seq, 512)
    return pl.pallas_call(
        functools.partial(_modulate_kernel, nl=seq // tl),
        grid=(m // tl,),
        in_specs=[pl.BlockSpec((tl, d), lambda i: (i, 0)), _mod_spec(d, sc), _mod_spec(d, sh)],
        out_specs=pl.BlockSpec((tl, d), lambda i: (i, 0)),
        out_shape=jax.ShapeDtypeStruct((m, d), BF16),
        compiler_params=_params("arbitrary"),
        name="modulate",
    )(x2, mod, mod)


def _mm_kernel(a_ref, w_ref, o_ref, w_s):
    @pl.when(pl.program_id(1) == 0)
    def _():
        w_s[...] = w_ref[0].astype(BF16)

    o_ref[...] = _dot(a_ref[...], w_s[...]).astype(o_ref.dtype)


def _mm(a, w, layer, n, out_dtype, tm, tn, name):
    m, k = a.shape
    tm = min(tm, m)
    tn = min(tn, n)
    return pl.pallas_call(
        _mm_kernel,
        grid=(n // tn, m // tm),
        in_specs=[pl.BlockSpec((tm, k), lambda j, i: (i, 0)),
                  pl.BlockSpec((1, k, tn), lambda j, i: (layer, 0, j))],
        out_specs=pl.BlockSpec((tm, tn), lambda j, i: (i, j)),
        out_shape=jax.ShapeDtypeStruct((m, n), out_dtype),
        scratch_shapes=[pltpu.VMEM((k, tn), BF16)],
        compiler_params=_params("arbitrary", "arbitrary"),
        name=name,
    )(a, w)


def _mm_t_kernel(wt_ref, a_ref, o_ref, w_s):
    @pl.when(pl.program_id(1) == 0)
    def _():
        w_s[...] = wt_ref[...].astype(BF16)

    o_ref[0] = _dot_nt(w_s[...], a_ref[...]).astype(o_ref.dtype)


def _mm_t(a, wt, bsz, seq, tn, tl, name):
    m, k = a.shape
    n = wt.shape[0]
    tl = min(tl, seq)
    nl = seq // tl
    return pl.pallas_call(
        _mm_t_kernel,
        grid=(n // tn, m // tl),
        in_specs=[pl.BlockSpec((tn, k), lambda j, i: (j, 0)),
                  pl.BlockSpec((tl, k), lambda j, i: (i, 0))],
        out_specs=pl.BlockSpec((1, tn, tl), lambda j, i: (i // nl, j, i % nl)),
        out_shape=jax.ShapeDtypeStruct((bsz, n, seq), BF16),
        scratch_shapes=[pltpu.VMEM((tn, k), BF16)],
        compiler_params=_params("arbitrary", "arbitrary"),
        name=name,
    )(wt, a)


def _gate_up_kernel(a_ref, wg_ref, wu_ref, o_ref, wg_s, wu_s):
    @pl.when(pl.program_id(1) == 0)
    def _():
        wg_s[...] = wg_ref[0].astype(BF16)
        wu_s[...] = wu_ref[0].astype(BF16)

    a = a_ref[...]
    g = _dot(a, wg_s[...])
    u = _dot(a, wu_s[...])
    o_ref[...] = (_silu(g) * u).astype(o_ref.dtype)


def _gate_up(a, wg, wu, layer, tm, tn):
    m, k = a.shape
    n = wg.shape[2]
    tm = min(tm, m)
    return pl.pallas_call(
        _gate_up_kernel,
        grid=(n // tn, m // tm),
        in_specs=[pl.BlockSpec((tm, k), lambda j, i: (i, 0)),
                  pl.BlockSpec((1, k, tn), lambda j, i: (layer, 0, j)),
                  pl.BlockSpec((1, k, tn), lambda j, i: (layer, 0, j))],
        out_specs=pl.BlockSpec((tm, tn), lambda j, i: (i, j)),
        out_shape=jax.ShapeDtypeStruct((m, n), BF16),
        scratch_shapes=[pltpu.VMEM((k, tn), BF16), pltpu.VMEM((k, tn), BF16)],
        compiler_params=_params("arbitrary", "arbitrary"),
        name="ffn_gate_up",
    )(a, wg, wu)


def _glu_kernel(y_ref, w_ref, o_ref):
    y = y_ref[...]
    t = _dot(y, w_ref[...])
    o_ref[...] = (y.astype(F32) * jax.nn.sigmoid(t)).astype(o_ref.dtype)


def _glu(y, w, tm):
    m, k = y.shape
    tm = min(tm, m)
    return pl.pallas_call(
        _glu_kernel,
        grid=(m // tm,),
        in_specs=[pl.BlockSpec((tm, k), lambda i: (i, 0)),
                  pl.BlockSpec((k, k), lambda i: (0, 0))],
        out_specs=pl.BlockSpec((tm, k), lambda i: (i, 0)),
        out_shape=jax.ShapeDtypeStruct((m, k), BF16),
        compiler_params=_params("arbitrary"),
        name="s5_glu",
    )(y, w)


def _res_ln_kernel(*refs, n_in, has_next, nl):
    a_refs = refs[:n_in]
    w_refs = refs[n_in:2 * n_in]
    x_ref, g_ref, lng_ref, lnb_ref = refs[2 * n_in:2 * n_in + 4]
    pos = 2 * n_in + 4
    if has_next:
        sc_ref, sh_ref = refs[pos:pos + 2]
        pos += 2
    xo_ref = refs[pos]
    pos += 1
    if has_next:
        ho_ref = refs[pos]
        pos += 1
    y_bufs = refs[pos:pos + 2]
    i = pl.program_id(0)
    b = jnp.maximum(i - 1, 0) // nl

    @pl.when(i == 0)
    def _():
        y_bufs[1][...] = jnp.zeros(y_bufs[1].shape, F32)

    def step(y_new, y_old):
        part = _dot(a_refs[0][...], w_refs[0][0])
        for a_ref, w_ref in zip(a_refs[1:], w_refs[1:]):
            part = part + _dot(a_ref[...], w_ref[0])
        y_new[...] = part
        r = ALPHA * x_ref[...] + (1.0 + g_ref[0, pl.ds(b, 1), :]) * y_old[...]
        mu = jnp.mean(r, axis=-1, keepdims=True)
        rc = r - mu
        var = jnp.mean(rc * rc, axis=-1, keepdims=True)
        xn = rc * lax.rsqrt(var + EPS) * lng_ref[...] + lnb_ref[...]
        xo_ref[...] = xn
        if has_next:
            ho_ref[...] = (xn * (1.0 + sc_ref[0, pl.ds(b, 1), :]) + sh_ref[0, pl.ds(b, 1), :]).astype(ho_ref.dtype)

    even = lax.rem(i, 2) == 0

    @pl.when(even)
    def _():
        step(y_bufs[0], y_bufs[1])

    @pl.when(jnp.logical_not(even))
    def _():
        step(y_bufs[1], y_bufs[0])


def _res_ln(a_list, w, layer, x2, mod, gate, ln_g, ln_b, nxt, seq, tm, name):
    m, d = x2.shape
    n_in = len(a_list)
    ka = a_list[0].shape[1]
    tm = min(tm, m, seq)
    nt = m // tm
    nl = seq // tm
    has_next = nxt is not None
    cur = lambda i: (jnp.minimum(i, nt - 1), 0)
    prev = lambda i: (jnp.maximum(i - 1, 0), 0)
    in_specs = [pl.BlockSpec((tm, ka), cur) for _ in a_list]
    in_specs += [pl.BlockSpec((1, ka, d), functools.partial(lambda i, s: (layer, s, 0), s=s),
                              pipeline_mode=pl.Buffered(1)) for s in range(n_in)]
    pvec = pl.BlockSpec((1, d), lambda i: (0, 0))
    in_specs += [pl.BlockSpec((tm, d), prev), _mod_spec(d, gate), pvec, pvec]
    args = list(a_list) + [w] * n_in + [x2, mod, ln_g.reshape(1, d), ln_b.reshape(1, d)]
    out_shape = [jax.ShapeDtypeStruct((m, d), F32)]
    out_specs = [pl.BlockSpec((tm, d), prev)]
    if has_next:
        in_specs += [_mod_spec(d, nxt[0]), _mod_spec(d, nxt[1])]
        args += [mod, mod]
        out_shape.append(jax.ShapeDtypeStruct((m, d), BF16))
        out_specs.append(pl.BlockSpec((tm, d), prev))
    out = pl.pallas_call(
        functools.partial(_res_ln_kernel, n_in=n_in, has_next=has_next, nl=nl),
        grid=(nt + 1,),
        in_specs=in_specs,
        out_specs=out_specs,
        out_shape=out_shape,
        scratch_shapes=[pltpu.VMEM((tm, d), F32), pltpu.VMEM((tm, d), F32)],
        compiler_params=_params("arbitrary"),
        name=name,
    )(*args)
    return (out[0], out[1]) if has_next else (out[0], None)


def _shift_rows(x, s, row):
    return jnp.where(row >= s, pltpu.roll(x, s, axis=0), 0.0)


def _conv_silu(x_ref, cw_ref):
    x = x_ref[...].astype(F32)
    w = cw_ref[...]
    row = lax.broadcasted_iota(jnp.int32, x.shape, 0)
    acc = x * w[A_CONV - 1:A_CONV, :]
    for s in range(1, A_CONV):
        acc = acc + _shift_rows(x, s, row) * w[A_CONV - 1 - s:A_CONV - s, :]
    return _silu(acc)


def _l2n(t):
    return t * lax.rsqrt(jnp.sum(t * t, axis=-1, keepdims=True) + 1e-6)


DN_HB = 4


def _delta_kernel(q_ref, k_ref, v_ref, z_ref, cwq_ref, cwk_ref, cwv_ref, a_ref, b_ref,
                  alog_ref, dtb_ref, ng_ref, o_ref,
                  q_s, k_s, v_s, gc_s, bt_s, u_s, wq_s, l2_s, gl_s, s_s, n_s, nt_s, x_s, *, nc, seq):
    c_len, dh = CHUNK, A_HEAD_DIM
    nmat_all = DN_HB * nc
    hh = lax.rem(pl.program_id(1), DN_HB)
    q_s[hh] = _l2n(_conv_silu(q_ref, cwq_ref)) * (dh ** -0.5)
    k_s[hh] = _l2n(_conv_silu(k_ref, cwk_ref))
    v_s[hh] = _conv_silu(v_ref, cwv_ref)

    a = a_ref[0, 0]
    b = b_ref[0, 0]
    sp_in = a + dtb_ref[0]
    softplus = jnp.maximum(sp_in, 0.0) + jnp.log1p(jnp.exp(-jnp.abs(sp_in)))
    g = -jnp.exp(alog_ref[0]) * softplus
    ri = lax.broadcasted_iota(jnp.int32, (c_len, LANES), 0)
    ci = lax.broadcasted_iota(jnp.int32, (c_len, LANES), 1)
    tril, strict, eye = ri >= ci, ri > ci, ri == ci
    eye64 = eye[:, :c_len]
    eye_bf = jnp.where(eye64, 1.0, 0.0).astype(BF16)
    g0, g1, g2 = _split3(g)
    triu = jnp.where(ci < c_len, jnp.where(ri <= ci, 1.0, 0.0), 0.0).astype(BF16)
    gc_s[hh] = _dot(g0, triu) + _dot(g1, triu) + _dot(g2, triu)
    bt_s[hh] = jax.nn.sigmoid(b)

    def prep(c, carry):
        r0 = pl.multiple_of(c * c_len, c_len)
        for hd in range(DN_HB):
            kc = k_s[hd, pl.ds(r0, c_len), :]
            qc = q_s[hd, pl.ds(r0, c_len), :]
            grow = gc_s[hd, pl.ds(c, 1), :]
            brow = bt_s[hd, pl.ds(c, 1), :]
            gcol = jnp.sum(jnp.where(eye, grow, 0.0), axis=1, keepdims=True)
            bcol = jnp.sum(jnp.where(eye64, brow, 0.0), axis=1, keepdims=True)
            decay = jnp.where(tril, jnp.exp(jnp.where(tril, gcol - grow, 0.0)), 0.0)
            kb = kc.astype(BF16)
            kk = _dot_nt(kb, jnp.concatenate([kb, jnp.zeros_like(kb)], axis=0))
            m0 = pl.multiple_of((hd * nc + c) * c_len, c_len)
            n_s[pl.ds(m0, c_len), :] = jnp.where(strict, kk * bcol * decay, 0.0)
            qk = _dot_nt(qc.astype(BF16), kb)
            glast = grow[:, c_len - 1:c_len]
            kt = (kc * jnp.exp(glast - gcol)).astype(BF16)
            wq_s[hd * nc + c, c_len:, :] = (qc * jnp.exp(gcol)).astype(BF16)
            l2_s[hd * nc + c, :c_len, :] = (qk * decay[:, :c_len]).astype(BF16)
            l2_s[hd * nc + c, c_len:, :] = _dot_tn(kt, eye_bf).astype(BF16)
            gl_s[hd, pl.ds(c, 1), :] = jnp.broadcast_to(jnp.exp(glast), (1, LANES))
        return carry

    @pl.when(hh == DN_HB - 1)
    def _():
        lax.fori_loop(0, nc, prep, 0)
        x_s[...] = jnp.zeros(x_s.shape, F32)
        for i in range(c_len):
            slab = n_s[pl.ds(i, nmat_all, stride=c_len), :]
            nt_s[i] = slab.T[:c_len, :]
        sub = lax.broadcasted_iota(jnp.int32, (8, LANES), 0)
        for ib in range(c_len // 8):
            npc = ib + 1

            def row(ii, carry, ib=ib, npc=npc):
                i = ib * 8 + ii
                parts = [[None] * 4 for _ in range(npc)]
                for j in range(8 * npc):
                    coef = nt_s[i, pl.ds(j, 1), :]
                    for p in range(j // 8 + 1):
                        term = coef * x_s[j, p * 8:(p + 1) * 8, :]
                        k = j % 4
                        parts[p][k] = term if parts[p][k] is None else parts[p][k] + term
                for p in range(npc):
                    live = [t for t in parts[p] if t is not None]
                    tot = live[0]
                    for t in live[1:]:
                        tot = tot + t
                    x_s[i, p * 8:(p + 1) * 8, :] = jnp.where(sub + p * 8 == i, 1.0, 0.0) - tot
                return carry

            lax.fori_loop(0, 8, row, 0)
        zpad = jnp.zeros((LANES - c_len, LANES), F32)
        for i in range(c_len):
            n_s[pl.ds(i, nmat_all, stride=c_len), :] = jnp.concatenate([x_s[i], zpad], axis=0).T

        def uw(hd, c):
            r0 = pl.multiple_of(c * c_len, c_len)
            idx = hd * nc + c
            t = n_s[pl.ds(pl.multiple_of(idx * c_len, c_len), c_len), :][:, :c_len]
            tb = t * bt_s[hd, pl.ds(c, 1), :]
            tw = tb * jnp.exp(gc_s[hd, pl.ds(c, 1), :][:, :c_len])
            kb = k_s[hd, pl.ds(r0, c_len), :].astype(BF16)
            vb = v_s[hd, pl.ds(r0, c_len), :].astype(BF16)
            u_s[hd, pl.ds(r0, c_len), :] = _dot(tb.astype(BF16), vb)
            wq_s[idx, :c_len, :] = _dot(tw.astype(BF16), kb).astype(BF16)

        for hd in range(DN_HB):
            uw(hd, 0)
        ng = ng_ref[...]
        s_s[...] = jnp.zeros(s_s.shape, F32)

        def scan(c, carry):
            r0 = pl.multiple_of(c * c_len, c_len)
            for hd in range(DN_HB):
                s = s_s[hd]
                sb = s.astype(BF16)
                r1 = _dot(wq_s[hd * nc + c], sb)
                v_new = u_s[hd, pl.ds(r0, c_len), :] - r1[:c_len]
                r2 = _dot(l2_s[hd * nc + c], v_new.astype(BF16))
                o = r1[c_len:] + r2[:c_len]
                s_s[hd] = s * gl_s[hd, pl.ds(c, 1), :] + r2[c_len:]
                q_s[hd, pl.ds(r0, c_len), :] = o
            cn = jnp.minimum(c + 1, nc - 1)
            for hd in range(DN_HB):
                uw(hd, cn)
            return carry

        lax.fori_loop(0, nc, scan, 0)

        rt = 256

        def gate(r, carry):
            r0 = pl.multiple_of(r * rt, rt)
            for hd in range(DN_HB):
                zc = z_ref[pl.ds(r0, rt), hd * dh:(hd + 1) * dh].astype(F32)
                o = q_s[hd, pl.ds(r0, rt), :]
                on = o * lax.rsqrt(jnp.mean(o * o, axis=-1, keepdims=True) + EPS) * ng
                o_ref[pl.ds(r0, rt), hd * dh:(hd + 1) * dh] = (on * _silu(zc)).astype(o_ref.dtype)
            return carry

        lax.fori_loop(0, seq // rt, gate, 0)


def _deltanet(proj, ba, conv_w, a_log, dt_bias, norm_g, bsz, seq):
    dh = A_HEAD_DIM
    heads = a_log.shape[0]
    nc = seq // CHUNK
    assert DN_HB * nc == LANES and heads % DN_HB == 0
    bh = ba[:, :2 * heads].reshape(bsz, seq, 2, heads).transpose(2, 0, 3, 1).reshape(2, bsz, heads, nc, CHUNK)
    col = lambda off: pl.BlockSpec((seq, dh), lambda b, h, off=off: (b, off + h))
    cw = lambda off: pl.BlockSpec((A_CONV, dh), lambda b, h, off=off: (0, off + h))
    rows = pl.BlockSpec((1, 1, nc, CHUNK), lambda b, h: (b, h, 0, 0))
    scal = pl.BlockSpec((1, 1, 1), lambda b, h: (h, 0, 0))
    hgroups = heads // DN_HB
    return pl.pallas_call(
        functools.partial(_delta_kernel, nc=nc, seq=seq),
        grid=(bsz, heads),
        in_specs=[col(0), col(heads), col(2 * heads),
                  pl.BlockSpec((seq, DN_HB * dh), lambda b, h: (b, 3 * hgroups + h // DN_HB)),
                  cw(0), cw(heads), cw(2 * heads), rows, rows, scal, scal,
                  pl.BlockSpec((1, dh), lambda b, h: (0, 0))],
        out_specs=pl.BlockSpec((seq, DN_HB * dh), lambda b, h: (b, h // DN_HB)),
        out_shape=jax.ShapeDtypeStruct((bsz * seq, heads * dh), BF16),
        scratch_shapes=[pltpu.VMEM((DN_HB, seq, dh), F32), pltpu.VMEM((DN_HB, seq, dh), F32),
                        pltpu.VMEM((DN_HB, seq, dh), F32),
                        pltpu.VMEM((DN_HB, nc, LANES), F32), pltpu.VMEM((DN_HB, nc, CHUNK), F32),
                        pltpu.VMEM((DN_HB, seq, dh), F32),
                        pltpu.VMEM((DN_HB * nc, 2 * CHUNK, dh), BF16), pltpu.VMEM((DN_HB * nc, CHUNK + dh, CHUNK), BF16),
                        pltpu.VMEM((DN_HB, nc, LANES), F32), pltpu.VMEM((DN_HB, dh, dh), F32),
                        pltpu.VMEM((DN_HB * nc * CHUNK, LANES), F32), pltpu.VMEM((CHUNK, CHUNK, LANES), F32),
                        pltpu.VMEM((CHUNK, CHUNK, LANES), F32)],
        compiler_params=_params("arbitrary", "arbitrary"),
        name="gated_deltanet",
    )(proj, proj, proj, proj, conv_w, conv_w, conv_w, bh[1], bh[0],
      a_log.reshape(heads, 1, 1), dt_bias.reshape(heads, 1, 1), norm_g.reshape(1, dh))


def _cexp(re, im):
    m = jnp.exp(re)
    return m * jnp.cos(im), m * jnp.sin(im)


def _cmul(ar, ai, br, bi):
    return ar * br - ai * bi, ar * bi + ai * br


def _gelu_tanh(x):
    return 0.5 * x * (1.0 + jnp.tanh(math.sqrt(2.0 / math.pi) * (x + 0.044715 * (x * x * x))))


S5_GB = LANES // S5_GROUP


def _s5_kernel(u_ref, bre_ref, bim_ref, cre_ref, cim_ref, lr_r_ref, li_r_ref, dt_r_ref,
               lr_c_ref, li_c_ref, dt_c_ref, d_ref, o_ref, ua_s, bps_s, cq_s, ks_s, y_s, *, nc):
    step, grp, st = S5_STEP, S5_GROUP, S5_STATE
    gw = S5_GB * grp
    half = S5_GB * st
    sw = 2 * half
    r = ua_s.shape[0]
    gsh, ssh = grp.bit_length() - 1, st.bit_length() - 1

    lr, li = lr_r_ref[0], li_r_ref[0]
    dt = jnp.exp(dt_r_ref[0])
    ar, ai = _cexp(lr * dt, li * dt)
    den = lr * lr + li * li
    pr = ((ar - 1.0) * lr + ai * li) / den
    pi_ = (ai * lr - (ar - 1.0) * li) / den
    re_lane = lax.broadcasted_iota(jnp.int32, (1, sw), 1) < half
    row_g = lax.shift_right_logical(lax.broadcasted_iota(jnp.int32, (gw, sw), 0), gsh)
    lane_g = lax.shift_right_logical(lax.broadcasted_iota(jnp.int32, (gw, sw), 1) & (half - 1), ssh)
    same_g = row_g == lane_g
    br = jnp.where(same_g, jnp.concatenate([bre_ref[0]] * (sw // LANES), axis=1), 0.0)
    bi = jnp.where(same_g, jnp.concatenate([bim_ref[0]] * (sw // LANES), axis=1), 0.0)
    bp0 = None
    for k in range(step):
        blk = br * jnp.where(re_lane, pr, pi_) + bi * jnp.where(re_lane, -pi_, pr)
        if k == 0:
            bp0 = blk
        s = step - 1 - k
        bps_s[s * gw:(s + 1) * gw, :] = blk.astype(BF16)
        pr, pi_ = _cmul(pr, pi_, ar, ai)

    dtc = jnp.exp(dt_c_ref[0])
    acr_d, aci_d = _cexp(lr_c_ref[0] * dtc, li_c_ref[0] * dtc)
    hr = half // LANES

    def spread(v):
        vt = jnp.concatenate([v, jnp.zeros((LANES - hr, LANES), F32)], axis=0).T
        return jnp.concatenate([jnp.broadcast_to(vt[:, q:q + 1], (LANES, gw)) for q in range(hr)], axis=0)

    acr, aci = spread(acr_d), spread(aci_d)
    same_c = (lax.shift_right_logical(lax.broadcasted_iota(jnp.int32, (half, gw), 0), ssh)
              == lax.shift_right_logical(lax.broadcasted_iota(jnp.int32, (half, gw), 1), gsh))
    pcr = jnp.where(same_c, cre_ref[0], 0.0)
    pci = jnp.where(same_c, cim_ref[0], 0.0)
    bp_hi = bp0.astype(BF16)
    bp_lo = (bp0 - bp_hi.astype(F32)).astype(BF16)
    for k in range(step + 1):
        re_hi, im_hi = pcr.astype(BF16), (-pci).astype(BF16)
        cq_hi = jnp.concatenate([re_hi, im_hi], axis=0)
        if k < step:
            cq_lo = jnp.concatenate([(pcr - re_hi.astype(F32)).astype(BF16),
                                     (-pci - im_hi.astype(F32)).astype(BF16)], axis=0)
            taps = _dot(bp_hi, cq_hi) + (_dot(bp_hi, cq_lo) + _dot(bp_lo, cq_hi))
            ks_s[(step - 1 - k) * gw:(step - k) * gw, :] = taps.astype(BF16)
            pcr, pci = _cmul(pcr, pci, acr, aci)
        if k >= 1:
            cq_s[k - 1] = cq_hi

    for s in range(step):
        ua_s[:, s * gw:(s + 1) * gw] = u_ref[pl.ds(s, r, stride=step), :].astype(BF16)
    x = _dot(ua_s[...], bps_s[...])
    blk_i = lax.broadcasted_iota(jnp.int32, (r, sw), 0) & (nc - 1)
    er, ei = ar, ai
    for _ in range(step.bit_length() - 1):
        er, ei = _cmul(er, ei, er, ei)
    sh = 1
    while sh < nc:
        xs = jnp.where(blk_i >= sh, pltpu.roll(x, sh, axis=0), 0.0)
        x = x + xs * er + pltpu.roll(xs, half, axis=1) * jnp.where(re_lane, -ei, ei)
        er, ei = _cmul(er, ei, er, ei)
        sh *= 2
    xp = jnp.where(blk_i >= 1, pltpu.roll(x, 1, axis=0), 0.0).astype(BF16)

    d = d_ref[0]
    for t in range(step):
        y = (_dot(ua_s[:, :(t + 1) * gw], ks_s[(step - 1 - t) * gw:, :]) + _dot(xp, cq_s[t])
             + d * u_ref[pl.ds(t, r, stride=step), :])
        y_s[pl.ds(t, r, stride=step), :] = _gelu_tanh(y)

    rt = 512

    def emit(i, carry):
        r0 = pl.multiple_of(i * rt, rt)
        o_ref[pl.ds(r0, rt), :] = y_s[pl.ds(r0, rt), :].astype(o_ref.dtype)
        return carry

    lax.fori_loop(0, y_s.shape[0] // rt, emit, 0)


def _s5(u2, a_re, a_im, b_re, b_im, c_re, c_im, d, log_dt, seq):
    groups, st = a_re.shape
    grp, step = S5_GROUP, S5_STEP
    m = u2.shape[0]
    g8 = groups // S5_GB
    gw, half = S5_GB * grp, S5_GB * st
    sw = 2 * half
    nc = seq // step
    assert nc & (nc - 1) == 0 and gw == LANES
    r = m // step
    bt = lambda t: jnp.tile(t.transpose(0, 2, 1).reshape(g8, gw, st), (1, 1, 2))
    ct = lambda t: jnp.tile(t.transpose(0, 2, 1).reshape(g8, half, grp), (1, 1, S5_GB))
    rowv = lambda t: jnp.tile(t.reshape(g8, 1, half), (1, 1, 2))
    colv = lambda t: t.reshape(g8, half // LANES, LANES)
    ldt = jnp.broadcast_to(log_dt[:, None], (groups, st))
    g3 = lambda shp: pl.BlockSpec((1,) + shp, lambda g: (g, 0, 0))
    dense = g3((half // LANES, LANES))
    return pl.pallas_call(
        functools.partial(_s5_kernel, nc=nc),
        grid=(g8,),
        in_specs=[pl.BlockSpec((m, gw), lambda g: (0, g)),
                  g3((gw, 2 * st)), g3((gw, 2 * st)), g3((half, gw)), g3((half, gw)),
                  g3((1, sw)), g3((1, sw)), g3((1, sw)), dense, dense, dense,
                  g3((1, gw))],
        out_specs=pl.BlockSpec((m, gw), lambda g: (0, g)),
        out_shape=jax.ShapeDtypeStruct((m, groups * grp), BF16),
        scratch_shapes=[pltpu.VMEM((r, step * gw), BF16), pltpu.VMEM((step * gw, sw), BF16),
                        pltpu.VMEM((step, sw, gw), BF16), pltpu.VMEM((step * gw, gw), BF16),
                        pltpu.VMEM((m, gw), F32)],
        compiler_params=_params("arbitrary"),
        name="s5_ssm",
    )(u2, bt(b_re), bt(b_im), ct(c_re), ct(c_im), rowv(a_re), rowv(a_im), rowv(ldt),
      colv(a_re), colv(a_im), colv(ldt), d.reshape(g8, 1, gw))


def _sconv_kernel(gb_ref, gc_ref, xc_ref, w_ref, o_ref):
    p = gc_ref[...].astype(F32) * xc_ref[...].astype(F32)
    w = w_ref[...]
    row = lax.broadcasted_iota(jnp.int32, p.shape, 0)
    acc = p * w[C_CONV - 1:C_CONV, :]
    for s in range(1, C_CONV):
        acc = acc + _shift_rows(p, s, row) * w[C_CONV - 1 - s:C_CONV - s, :]
    o_ref[...] = (gb_ref[...].astype(F32) * acc).astype(o_ref.dtype)


def _sconv(proj, conv_w, bsz, seq):
    width = conv_w.shape[1]
    tc = 256
    nb = width // tc
    col = lambda off: pl.BlockSpec((seq, tc), lambda b, j, off=off: (b, off * nb + j))
    return pl.pallas_call(
        _sconv_kernel,
        grid=(bsz, nb),
        in_specs=[col(0), col(1), col(2), pl.BlockSpec((C_CONV, tc), lambda b, j: (0, j))],
        out_specs=pl.BlockSpec((seq, tc), lambda b, j: (b, j)),
        out_shape=jax.ShapeDtypeStruct((bsz * seq, width), BF16),
        compiler_params=_params("arbitrary", "arbitrary"),
        name="short_conv",
    )(proj, proj, proj, conv_w)


NEG_BIG = -1e30
LOG2_E = 1.4426950408889634


def _dattn_kernel(q_ref, k_ref, vt_ref, lq1_ref, lk1_ref, lq2_ref, lk2_ref, ng_ref, o_ref,
                  acc_s, *, heads, tq, lambda_init):
    dh = DA_HEAD_DIM
    qi = pl.program_id(1)
    scale = dh ** -0.5
    lam = (jnp.exp(jnp.sum(lq1_ref[...] * lk1_ref[...], axis=-1, keepdims=True))
           - jnp.exp(jnp.sum(lq2_ref[...] * lk2_ref[...], axis=-1, keepdims=True)) + lambda_init)
    ki = lax.broadcasted_iota(jnp.int32, (tq, tq), 0)
    qj = lax.broadcasted_iota(jnp.int32, (tq, tq), 1)
    kpos = ki.astype(F32)
    visible = ki <= qj
    ng = ng_ref[...]

    hpi = 2
    for h0 in range(0, heads, hpi):
        slopes = [2.0 ** (-8.0 * (h + 1) / heads) for h in range(h0, h0 + hpi)]

        key_bias = [(slopes[t] * LOG2_E) * kpos for t in range(hpi)]

        def block(j, carry, masked, h0=h0, slopes=slopes, key_bias=key_bias):
            k0 = pl.multiple_of(j * tq, tq)
            k0f = k0.astype(F32)
            out = []
            for t in range(hpi):
                v_lo = (h0 + t) * 2 * dh
                vt = vt_ref[0, v_lo:v_lo + 2 * dh, pl.ds(k0, tq)]
                off = (slopes[t] * LOG2_E) * k0f
                for mp in range(2):
                    c = 2 * t + mp
                    m, l = carry[2 * c], carry[2 * c + 1]
                    lo = v_lo + mp * dh
                    s = (_dot_nt(k_ref[pl.ds(k0, tq), lo:lo + dh], q_ref[:, lo:lo + dh]) * (scale * LOG2_E)
                         + key_bias[t])
                    if masked:
                        s = jnp.where(visible, s, NEG_BIG)
                    m_new = jnp.maximum(m, jnp.max(s, axis=0, keepdims=True) + off)
                    corr = jnp.exp2(m - m_new)
                    p = jnp.exp2(s + (off - m_new))
                    out += [m_new, corr * l + jnp.sum(p, axis=0, keepdims=True)]
                    acc_s[c] = corr * acc_s[c] + _dot(vt, p.astype(BF16))
            return tuple(out)

        acc_s[...] = jnp.zeros(acc_s.shape, F32)
        neg = jnp.full((1, tq), NEG_BIG, F32)
        zero = jnp.zeros((1, tq), F32)
        carry = lax.fori_loop(0, qi, functools.partial(block, masked=False), (neg, zero) * (2 * hpi))
        carry = block(qi, carry, True)
        for t in range(hpi):
            v_lo = (h0 + t) * 2 * dh
            l1, l2 = carry[4 * t + 1], carry[4 * t + 3]
            o = acc_s[2 * t] / l1 - lam * (acc_s[2 * t + 1] / l2)
            o = o * lax.rsqrt(jnp.mean(o * o, axis=0, keepdims=True) + EPS)
            o_ref[:, v_lo:v_lo + 2 * dh] = (o.T * ng * (1.0 - lambda_init)).astype(o_ref.dtype)


def _dattn(proj, vt, lq1, lk1, lq2, lk2, norm_g, lambda_init, bsz, seq):
    dh = DA_HEAD_DIM
    width = vt.shape[1]
    heads = width // (2 * dh)
    tq = min(256, seq)
    nq = seq // tq
    vec = pl.BlockSpec((1, dh), lambda b, i: (0, 0))
    return pl.pallas_call(
        functools.partial(_dattn_kernel, heads=heads, tq=tq, lambda_init=lambda_init),
        grid=(bsz, nq),
        in_specs=[pl.BlockSpec((tq, width), lambda b, i: (b * nq + i, 3)),
                  pl.BlockSpec((seq, width), lambda b, i: (b, 4)),
                  pl.BlockSpec((1, width, seq), lambda b, i: (b, 0, 0)),
                  vec, vec, vec, vec, pl.BlockSpec((1, 2 * dh), lambda b, i: (0, 0))],
        out_specs=pl.BlockSpec((tq, width), lambda b, i: (b * nq + i, 0)),
        out_shape=jax.ShapeDtypeStruct((bsz * seq, width), BF16),
        scratch_shapes=[pltpu.VMEM((4, 2 * dh, tq), F32)],
        compiler_params=_params("arbitrary", "arbitrary"),
        name="diff_attention",
    )(proj, proj, vt, lq1.reshape(1, dh), lk1.reshape(1, dh), lq2.reshape(1, dh), lk2.reshape(1, dh),
      norm_g.reshape(1, 2 * dh))


def kernel(x, c, ada_w, ada_b, ln1_g, ln1_b, ln2_g, ln2_b, ffn_w_gate, ffn_w_up, ffn_w_down, ab_w_in, ab_w_out, dn_conv_w, dn_a_log, dn_dt_bias, dn_norm_g, s5_a_re, s5_a_im, s5_b_re, s5_b_im, s5_c_re, s5_c_im, s5_d, s5_log_dt, s5_w_glu, cd_w_in, cd_w_out, sc_conv_w, da_lq1, da_lk1, da_lq2, da_lk2, da_norm_g):
    bsz, seq, d = x.shape
    depth = ada_w.shape[0]
    m = bsz * seq
    a_width = dn_conv_w.shape[-1] // 3
    heads = dn_a_log.shape[-1]

    c_pad = jnp.zeros((8, d), F32).at[:bsz].set(c)
    mod = _ada_mod(c_pad, ada_w, ada_b)

    ab_out, cd_out, w_down = ab_w_out.astype(BF16), cd_w_out.astype(BF16), ffn_w_down.astype(BF16)
    x2 = x.reshape(m, d)
    h = _modulate(x2, mod, (0, 1), (0, 0), seq)
    for i in range(depth):
        j = i // 2
        if i % 2 == 0:
            w_uba = jnp.pad(jnp.concatenate([ab_w_in[j:j + 1, :, 4 * a_width + 2 * heads:],
                                             ab_w_in[j:j + 1, :, 4 * a_width:4 * a_width + 2 * heads]], axis=2),
                            ((0, 0), (0, 0), (0, LANES - 2 * heads)))
            b_width = w_uba.shape[2] - LANES
            proj = _mm(h, ab_w_in, j, 4 * a_width, BF16, 1024, 1024, "ab_in_proj")
            uf = _mm(h, w_uba, 0, w_uba.shape[2], F32, 1024, w_uba.shape[2], "ab_u_proj")
            ya = _deltanet(proj, uf[:, b_width:], dn_conv_w[j], dn_a_log[j], dn_dt_bias[j], dn_norm_g[j], bsz, seq)
            yb = _s5(uf, s5_a_re[j], s5_a_im[j], s5_b_re[j], s5_b_im[j],
                     s5_c_re[j], s5_c_im[j], s5_d[j], s5_log_dt[j], seq)
            yb = _glu(yb, s5_w_glu[j].astype(BF16), 1024)
            w_out = ab_out
        else:
            lambda_init = 0.8 - 0.6 * math.exp(-0.3 * i)
            c_width = sc_conv_w.shape[-1]
            proj = _mm(h, cd_w_in, j, 5 * c_width, BF16, 1024, 1024, "cd_in_proj")
            vt = _mm_t(h, cd_w_in[j, :, 5 * c_width:].T, bsz, seq, 512, 1024, "cd_value_proj")
            ya = _sconv(proj, sc_conv_w[j], bsz, seq)
            yb = _dattn(proj, vt, da_lq1[j], da_lk1[j], da_lq2[j], da_lk2[j], da_norm_g[j], lambda_init, bsz, seq)
            w_out = cd_out
        x2, h = _res_ln([ya, yb], w_out, j, x2, mod, (i, 2), ln1_g[i], ln1_b[i],
                        ((i, 4), (i, 3)), seq, 512, "mixer_out_ln")
        act = _gate_up(h, ffn_w_gate, ffn_w_up, i, 1024, 512)
        nxt = ((i + 1, 1), (i + 1, 0)) if i + 1 < depth else None
        x2, h = _res_ln([act], w_down, i, x2, mod, (i, 5), ln2_g[i], ln2_b[i],
                        nxt, seq, 256, "ffn_down_ln")
    return x2.reshape(bsz, seq, d)
```

```python
import functools
import math

import jax
import jax.numpy as jnp
from jax import lax
from jax.experimental import pallas as pl
from jax.experimental.pallas import tpu as pltpu

F32 = jnp.float32
BF16 = jnp.bfloat16

DEPTH = 4
A_HEAD_DIM = 128
A_CONV = 4
CHUNK = 64
S5_GROUP = 16
S5_STATE = 64
S5_STEP = 16
C_CONV = 3
DA_HEAD_DIM = 128
ALPHA = (2.0 * DEPTH) ** 0.25
EPS = 1e-5
LANES = 128
VMEM_LIMIT_BYTES = 56 * 1024 * 1024


def _params(*sem):
    return pltpu.CompilerParams(dimension_semantics=sem, vmem_limit_bytes=VMEM_LIMIT_BYTES)


def _silu(x):
    return x * jax.nn.sigmoid(x)


def _dot(a, b):
    return jnp.dot(a, b, preferred_element_type=F32)


def _dot_nt(a, b):
    return lax.dot_general(a, b, (((1,), (1,)), ((), ())), preferred_element_type=F32)


def _dot_tn(a, b):
    return lax.dot_general(a, b, (((0,), (0,)), ((), ())), preferred_element_type=F32)


def _split3(x):
    hi = x.astype(BF16)
    r1 = x - hi.astype(F32)
    mid = r1.astype(BF16)
    lo = (r1 - mid.astype(F32)).astype(BF16)
    return hi, mid, lo


def _dot_f32(a, b):
    a0, a1, a2 = _split3(a)
    b0, b1, b2 = _split3(b)
    return (_dot(a0, b0) + (_dot(a0, b1) + _dot(a1, b0))
            + (_dot(a1, b1) + _dot(a0, b2) + _dot(a2, b0)))


def _ada_kernel(c_ref, w_ref, b_ref, o_ref):
    c = c_ref[...]
    ca = _silu(c).astype(BF16)
    o_ref[0] = _dot(ca, w_ref[0].astype(BF16)) + b_ref[0]


def _ada_mod(c_pad, ada_w, ada_b):
    depth, d, n = ada_w.shape
    rows = c_pad.shape[0]
    tn = 1024
    return pl.pallas_call(
        _ada_kernel,
        grid=(depth, n // tn),
        in_specs=[pl.BlockSpec((rows, d), lambda i, j: (0, 0)),
                  pl.BlockSpec((1, d, tn), lambda i, j: (i, 0, j)),
                  pl.BlockSpec((1, 1, tn), lambda i, j: (i, 0, j))],
        out_specs=pl.BlockSpec((1, rows, tn), lambda i, j: (i, 0, j)),
        out_shape=jax.ShapeDtypeStruct((depth, rows, n), F32),
        compiler_params=_params("arbitrary", "arbitrary"),
        name="ada_mod",
    )(c_pad, ada_w, ada_b.reshape(depth, 1, n))


def _mod_spec(d, sel):
    layer, col = sel
    return pl.BlockSpec((1, 8, d), lambda *_: (layer, 0, col))


def _modulate_kernel(x_ref, sc_ref, sh_ref, o_ref, *, nl):
    b = pl.program_id(0) // nl
    o_ref[...] = (x_ref[...] * (1.0 + sc_ref[0, pl.ds(b, 1), :]) + sh_ref[0, pl.ds(b, 1), :]).astype(o_ref.dtype)


def _modulate(x2, mod, sc, sh, seq):
    m, d = x2.shape
    tl = min(seq, 512)
    return pl.pallas_call(
        functools.partial(_modulate_kernel, nl=seq // tl),
        grid=(m // tl,),
        in_specs=[pl.BlockSpec((tl, d), lambda i: (i, 0)), _mod_spec(d, sc), _mod_spec(d, sh)],
        out_specs=pl.BlockSpec((tl, d), lambda i: (i, 0)),
        out_shape=jax.ShapeDtypeStruct((m, d), BF16),
        compiler_params=_params("arbitrary"),
        name="modulate",
    )(x2, mod, mod)


def _mm_kernel(a_ref, w_ref, o_ref, w_s):
    @pl.when(pl.program_id(1) == 0)
    def _():
        w_s[...] = w_ref[0].astype(BF16)

    o_ref[...] = _dot(a_ref[...], w_s[...]).astype(o_ref.dtype)


def _mm(a, w, layer, n, out_dtype, tm, tn, name):
    m, k = a.shape
    tm = min(tm, m)
    tn = min(tn, n)
    return pl.pallas_call(
        _mm_kernel,
        grid=(n // tn, m // tm),
        in_specs=[pl.BlockSpec((tm, k), lambda j, i: (i, 0)),
                  pl.BlockSpec((1, k, tn), lambda j, i: (layer, 0, j))],
        out_specs=pl.BlockSpec((tm, tn), lambda j, i: (i, j)),
        out_shape=jax.ShapeDtypeStruct((m, n), out_dtype),
        scratch_shapes=[pltpu.VMEM((k, tn), BF16)],
        compiler_params=_params("arbitrary", "arbitrary"),
        name=name,
    )(a, w)


def _mm_t_kernel(wt_ref, a_ref, o_ref, w_s):
    @pl.when(pl.program_id(1) == 0)
    def _():
        w_s[...] = wt_ref[...].astype(BF16)

    o_ref[0] = _dot_nt(w_s[...], a_ref[...]).astype(o_ref.dtype)


def _mm_t(a, wt, bsz, seq, tn, tl, name):
    m, k = a.shape
    n = wt.shape[0]
    tl = min(tl, seq)
    nl = seq // tl
    return pl.pallas_call(
        _mm_t_kernel,
        grid=(n // tn, m // tl),
        in_specs=[pl.BlockSpec((tn, k), lambda j, i: (j, 0)),
                  pl.BlockSpec((tl, k), lambda j, i: (i, 0))],
        out_specs=pl.BlockSpec((1, tn, tl), lambda j, i: (i // nl, j, i % nl)),
        out_shape=jax.ShapeDtypeStruct((bsz, n, seq), BF16),
        scratch_shapes=[pltpu.VMEM((tn, k), BF16)],
        compiler_params=_params("arbitrary", "arbitrary"),
        name=name,
    )(wt, a)


def _gate_up_kernel(a_ref, wg_ref, wu_ref, o_ref, wg_s, wu_s):
    @pl.when(pl.program_id(1) == 0)
    def _():
        wg_s[...] = wg_ref[0].astype(BF16)
        wu_s[...] = wu_ref[0].astype(BF16)

    a = a_ref[...]
    g = _dot(a, wg_s[...])
    u = _dot(a, wu_s[...])
    o_ref[...] = (_silu(g) * u).astype(o_ref.dtype)


def _gate_up(a, wg, wu, layer, tm, tn):
    m, k = a.shape
    n = wg.shape[2]
    tm = min(tm, m)
    return pl.pallas_call(
        _gate_up_kernel,
        grid=(n // tn, m // tm),
        in_specs=[pl.BlockSpec((tm, k), lambda j, i: (i, 0)),
                  pl.BlockSpec((1, k, tn), lambda j, i: (layer, 0, j)),
                  pl.BlockSpec((1, k, tn), lambda j, i: (layer, 0, j))],
        out_specs=pl.BlockSpec((tm, tn), lambda j, i: (i, j)),
        out_shape=jax.ShapeDtypeStruct((m, n), BF16),
        scratch_shapes=[pltpu.VMEM((k, tn), BF16), pltpu.VMEM((k, tn), BF16)],
        compiler_params=_params("arbitrary", "arbitrary"),
        name="ffn_gate_up",
    )(a, wg, wu)


def _glu_kernel(y_ref, w_ref, o_ref):
    y = y_ref[...]
    t = _dot(y, w_ref[...])
    o_ref[...] = (y.astype(F32) * jax.nn.sigmoid(t)).astype(o_ref.dtype)


def _glu(y, w, tm):
    m, k = y.shape
    tm = min(tm, m)
    return pl.pallas_call(
        _glu_kernel,
        grid=(m // tm,),
        in_specs=[pl.BlockSpec((tm, k), lambda i: (i, 0)),
                  pl.BlockSpec((k, k), lambda i: (0, 0))],
        out_specs=pl.BlockSpec((tm, k), lambda i: (i, 0)),
        out_shape=jax.ShapeDtypeStruct((m, k), BF16),
        compiler_params=_params("arbitrary"),
        name="s5_glu",
    )(y, w)


def _res_ln_kernel(*refs, n_in, has_next, nl):
    a_refs = refs[:n_in]
    w_refs = refs[n_in:2 * n_in]
    x_ref, g_ref, lng_ref, lnb_ref = refs[2 * n_in:2 * n_in + 4]
    pos = 2 * n_in + 4
    if has_next:
        sc_ref, sh_ref = refs[pos:pos + 2]
        pos += 2
    xo_ref = refs[pos]
    pos += 1
    if has_next:
        ho_ref = refs[pos]
        pos += 1
    y_bufs = refs[pos:pos + 2]
    i = pl.program_id(0)
    b = jnp.maximum(i - 1, 0) // nl

    @pl.when(i == 0)
    def _():
        y_bufs[1][...] = jnp.zeros(y_bufs[1].shape, F32)

    def step(y_new, y_old):
        part = _dot(a_refs[0][...], w_refs[0][0])
        for a_ref, w_ref in zip(a_refs[1:], w_refs[1:]):
            part = part + _dot(a_ref[...], w_ref[0])
        y_new[...] = part
        r = ALPHA * x_ref[...] + (1.0 + g_ref[0, pl.ds(b, 1), :]) * y_old[...]
        mu = jnp.mean(r, axis=-1, keepdims=True)
        rc = r - mu
        var = jnp.mean(rc * rc, axis=-1, keepdims=True)
        xn = rc * lax.rsqrt(var + EPS) * lng_ref[...] + lnb_ref[...]
        xo_ref[...] = xn
        if has_next:
            ho_ref[...] = (xn * (1.0 + sc_ref[0, pl.ds(b, 1), :]) + sh_ref[0, pl.ds(b, 1), :]).astype(ho_ref.dtype)

    even = lax.rem(i, 2) == 0

    @pl.when(even)
    def _():
        step(y_bufs[0], y_bufs[1])

    @pl.when(jnp.logical_not(even))
    def _():
        step(y_bufs[1], y_bufs[0])


def _res_ln(a_list, w, layer, x2, mod, gate, ln_g, ln_b, nxt, seq, tm, name):
    m, d = x2.shape
    n_in = len(a_list)
    ka = a_list[0].shape[1]
    tm = min(tm, m, seq)
    nt = m // tm
    nl = seq // tm
    has_next = nxt is not None
    cur = lambda i: (jnp.minimum(i, nt - 1), 0)
    prev = lambda i: (jnp.maximum(i - 1, 0), 0)
    in_specs = [pl.BlockSpec((tm, ka), cur) for _ in a_list]
    in_specs += [pl.BlockSpec((1, ka, d), functools.partial(lambda i, s: (layer, s, 0), s=s),
                              pipeline_mode=pl.Buffered(1)) for s in range(n_in)]
    pvec = pl.BlockSpec((1, d), lambda i: (0, 0))
    in_specs += [pl.BlockSpec((tm, d), prev), _mod_spec(d, gate), pvec, pvec]
    args = list(a_list) + [w] * n_in + [x2, mod, ln_g.reshape(1, d), ln_b.reshape(1, d)]
    out_shape = [jax.ShapeDtypeStruct((m, d), F32)]
    out_specs = [pl.BlockSpec((tm, d), prev)]
    if has_next:
        in_specs += [_mod_spec(d, nxt[0]), _mod_spec(d, nxt[1])]
        args += [mod, mod]
        out_shape.append(jax.ShapeDtypeStruct((m, d), BF16))
        out_specs.append(pl.BlockSpec((tm, d), prev))
    out = pl.pallas_call(
        functools.partial(_res_ln_kernel, n_in=n_in, has_next=has_next, nl=nl),
        grid=(nt + 1,),
        in_specs=in_specs,
        out_specs=out_specs,
        out_shape=out_shape,
        scratch_shapes=[pltpu.VMEM((tm, d), F32), pltpu.VMEM((tm, d), F32)],
        compiler_params=_params("arbitrary"),
        name=name,
    )(*args)
    return (out[0], out[1]) if has_next else (out[0], None)


def _shift_rows(x, s, row):
    return jnp.where(row >= s, pltpu.roll(x, s, axis=0), 0.0)


def _conv_silu(x_ref, cw_ref):
    x = x_ref[...].astype(F32)
    w = cw_ref[...]
    row = lax.broadcasted_iota(jnp.int32, x.shape, 0)
    acc = x * w[A_CONV - 1:A_CONV, :]
    for s in range(1, A_CONV):
        acc = acc + _shift_rows(x, s, row) * w[A_CONV - 1 - s:A_CONV - s, :]
    return _silu(acc)


def _l2n(t):
    return t * lax.rsqrt(jnp.sum(t * t, axis=-1, keepdims=True) + 1e-6)


DN_HB = 4


def _delta_kernel(q_ref, k_ref, v_ref, z_ref, cwq_ref, cwk_ref, cwv_ref, a_ref, b_ref,
                  alog_ref, dtb_ref, ng_ref, o_ref,
                  q_s, k_s, v_s, gc_s, bt_s, u_s, wq_s, l2_s, gl_s, s_s, n_s, nt_s, x_s, *, nc, seq):
    c_len, dh = CHUNK, A_HEAD_DIM
    nmat_all = DN_HB * nc
    hh = lax.rem(pl.program_id(1), DN_HB)
    q_s[hh] = _l2n(_conv_silu(q_ref, cwq_ref)) * (dh ** -0.5)
    k_s[hh] = _l2n(_conv_silu(k_ref, cwk_ref))
    v_s[hh] = _conv_silu(v_ref, cwv_ref)

    a = a_ref[0, 0]
    b = b_ref[0, 0]
    sp_in = a + dtb_ref[0]
    softplus = jnp.maximum(sp_in, 0.0) + jnp.log1p(jnp.exp(-jnp.abs(sp_in)))
    g = -jnp.exp(alog_ref[0]) * softplus
    ri = lax.broadcasted_iota(jnp.int32, (c_len, LANES), 0)
    ci = lax.broadcasted_iota(jnp.int32, (c_len, LANES), 1)
    tril, strict, eye = ri >= ci, ri > ci, ri == ci
    eye64 = eye[:, :c_len]
    eye_bf = jnp.where(eye64, 1.0, 0.0).astype(BF16)
    g0, g1, g2 = _split3(g)
    triu = jnp.where(ci < c_len, jnp.where(ri <= ci, 1.0, 0.0), 0.0).astype(BF16)
    gc_s[hh] = _dot(g0, triu) + _dot(g1, triu) + _dot(g2, triu)
    bt_s[hh] = jax.nn.sigmoid(b)

    def prep(c, carry):
        r0 = pl.multiple_of(c * c_len, c_len)
        for hd in range(DN_HB):
            kc = k_s[hd, pl.ds(r0, c_len), :]
            qc = q_s[hd, pl.ds(r0, c_len), :]
            grow = gc_s[hd, pl.ds(c, 1), :]
            brow = bt_s[hd, pl.ds(c, 1), :]
            gcol = jnp.sum(jnp.where(eye, grow, 0.0), axis=1, keepdims=True)
            bcol = jnp.sum(jnp.where(eye64, brow, 0.0), axis=1, keepdims=True)
            decay = jnp.where(tril, jnp.exp(jnp.where(tril, gcol - grow, 0.0)), 0.0)
            kb = kc.astype(BF16)
            kk = _dot_nt(kb, jnp.concatenate([kb, jnp.zeros_like(kb)], axis=0))
            m0 = pl.multiple_of((hd * nc + c) * c_len, c_len)
            n_s[pl.ds(m0, c_len), :] = jnp.where(strict, kk * bcol * decay, 0.0)
            qk = _dot_nt(qc.astype(BF16), kb)
            glast = grow[:, c_len - 1:c_len]
            kt = (kc * jnp.exp(glast - gcol)).astype(BF16)
            wq_s[hd * nc + c, c_len:, :] = (qc * jnp.exp(gcol)).astype(BF16)
            l2_s[hd * nc + c, :c_len, :] = (qk * decay[:, :c_len]).astype(BF16)
            l2_s[hd * nc + c, c_len:, :] = _dot_tn(kt, eye_bf).astype(BF16)
            gl_s[hd, pl.ds(c, 1), :] = jnp.broadcast_to(jnp.exp(glast), (1, LANES))
        return carry

    @pl.when(hh == DN_HB - 1)
    def _():
        lax.fori_loop(0, nc, prep, 0)
        x_s[...] = jnp.zeros(x_s.shape, F32)
        for i in range(c_len):
            slab = n_s[pl.ds(i, nmat_all, stride=c_len), :]
            nt_s[i] = slab.T[:c_len, :]
        sub = lax.broadcasted_iota(jnp.int32, (8, LANES), 0)
        for ib in range(c_len // 8):
            npc = ib + 1

            def row(ii, carry, ib=ib, npc=npc):
                i = ib * 8 + ii
                parts = [[None] * 4 for _ in range(npc)]
                for j in range(8 * npc):
                    coef = nt_s[i, pl.ds(j, 1), :]
                    for p in range(j // 8 + 1):
                        term = coef * x_s[j, p * 8:(p + 1) * 8, :]
                        k = j % 4
                        parts[p][k] = term if parts[p][k] is None else parts[p][k] + term
                for p in range(npc):
                    live = [t for t in parts[p] if t is not None]
                    tot = live[0]
                    for t in live[1:]:
                        tot = tot + t
                    x_s[i, p * 8:(p + 1) * 8, :] = jnp.where(sub + p * 8 == i, 1.0, 0.0) - tot
                return carry

            lax.fori_loop(0, 8, row, 0)
        zpad = jnp.zeros((LANES - c_len, LANES), F32)
        for i in range(c_len):
            n_s[pl.ds(i, nmat_all, stride=c_len), :] = jnp.concatenate([x_s[i], zpad], axis=0).T

        def uw(hd, c):
            r0 = pl.multiple_of(c * c_len, c_len)
            idx = hd * nc + c
            t = n_s[pl.ds(pl.multiple_of(idx * c_len, c_len), c_len), :][:, :c_len]
            tb = t * bt_s[hd, pl.ds(c, 1), :]
            tw = tb * jnp.exp(gc_s[hd, pl.ds(c, 1), :][:, :c_len])
            kb = k_s[hd, pl.ds(r0, c_len), :].astype(BF16)
            vb = v_s[hd, pl.ds(r0, c_len), :].astype(BF16)
            u_s[hd, pl.ds(r0, c_len), :] = _dot(tb.astype(BF16), vb)
            wq_s[idx, :c_len, :] = _dot(tw.astype(BF16), kb).astype(BF16)

        for hd in range(DN_HB):
            uw(hd, 0)
        ng = ng_ref[...]
        s_s[...] = jnp.zeros(s_s.shape, F32)

        def scan(c, carry):
            r0 = pl.multiple_of(c * c_len, c_len)
            for hd in range(DN_HB):
                s = s_s[hd]
                sb = s.astype(BF16)
                r1 = _dot(wq_s[hd * nc + c], sb)
                v_new = u_s[hd, pl.ds(r0, c_len), :] - r1[:c_len]
                r2 = _dot(l2_s[hd * nc + c], v_new.astype(BF16))
                o = r1[c_len:] + r2[:c_len]
                s_s[hd] = s * gl_s[hd, pl.ds(c, 1), :] + r2[c_len:]
                q_s[hd, pl.ds(r0, c_len), :] = o
            cn = jnp.minimum(c + 1, nc - 1)
            for hd in range(DN_HB):
                uw(hd, cn)
            return carry

        lax.fori_loop(0, nc, scan, 0)

        rt = 256

        def gate(r, carry):
            r0 = pl.multiple_of(r * rt, rt)
            for hd in range(DN_HB):
                zc = z_ref[pl.ds(r0, rt), hd * dh:(hd + 1) * dh].astype(F32)
                o = q_s[hd, pl.ds(r0, rt), :]
                on = o * lax.rsqrt(jnp.mean(o * o, axis=-1, keepdims=True) + EPS) * ng
                o_ref[pl.ds(r0, rt), hd * dh:(hd + 1) * dh] = (on * _silu(zc)).astype(o_ref.dtype)
            return carry

        lax.fori_loop(0, seq // rt, gate, 0)


def _deltanet(proj, ba, conv_w, a_log, dt_bias, norm_g, bsz, seq):
    dh = A_HEAD_DIM
    heads = a_log.shape[0]
    nc = seq // CHUNK
    assert DN_HB * nc == LANES and heads % DN_HB == 0
    bh = ba[:, :2 * heads].reshape(bsz, seq, 2, heads).transpose(2, 0, 3, 1).reshape(2, bsz, heads, nc, CHUNK)
    col = lambda off: pl.BlockSpec((seq, dh), lambda b, h, off=off: (b, off + h))
    cw = lambda off: pl.BlockSpec((A_CONV, dh), lambda b, h, off=off: (0, off + h))
    rows = pl.BlockSpec((1, 1, nc, CHUNK), lambda b, h: (b, h, 0, 0))
    scal = pl.BlockSpec((1, 1, 1), lambda b, h: (h, 0, 0))
    hgroups = heads // DN_HB
    return pl.pallas_call(
        functools.partial(_delta_kernel, nc=nc, seq=seq),
        grid=(bsz, heads),
        in_specs=[col(0), col(heads), col(2 * heads),
                  pl.BlockSpec((seq, DN_HB * dh), lambda b, h: (b, 3 * hgroups + h // DN_HB)),
                  cw(0), cw(heads), cw(2 * heads), rows, rows, scal, scal,
                  pl.BlockSpec((1, dh), lambda b, h: (0, 0))],
        out_specs=pl.BlockSpec((seq, DN_HB * dh), lambda b, h: (b, h // DN_HB)),
        out_shape=jax.ShapeDtypeStruct((bsz * seq, heads * dh), BF16),
        scratch_shapes=[pltpu.VMEM((DN_HB, seq, dh), F32), pltpu.VMEM((DN_HB, seq, dh), F32),
                        pltpu.VMEM((DN_HB, seq, dh), F32),
                        pltpu.VMEM((DN_HB, nc, LANES), F32), pltpu.VMEM((DN_HB, nc, CHUNK), F32),
                        pltpu.VMEM((DN_HB, seq, dh), F32),
                        pltpu.VMEM((DN_HB * nc, 2 * CHUNK, dh), BF16), pltpu.VMEM((DN_HB * nc, CHUNK + dh, CHUNK), BF16),
                        pltpu.VMEM((DN_HB, nc, LANES), F32), pltpu.VMEM((DN_HB, dh, dh), F32),
                        pltpu.VMEM((DN_HB * nc * CHUNK, LANES), F32), pltpu.VMEM((CHUNK, CHUNK, LANES), F32),
                        pltpu.VMEM((CHUNK, CHUNK, LANES), F32)],
        compiler_params=_params("arbitrary", "arbitrary"),
        name="gated_deltanet",
    )(proj, proj, proj, proj, conv_w, conv_w, conv_w, bh[1], bh[0],
      a_log.reshape(heads, 1, 1), dt_bias.reshape(heads, 1, 1), norm_g.reshape(1, dh))


def _cexp(re, im):
    m = jnp.exp(re)
    return m * jnp.cos(im), m * jnp.sin(im)


def _cmul(ar, ai, br, bi):
    return ar * br - ai * bi, ar * bi + ai * br


def _gelu_tanh(x):
    return 0.5 * x * (1.0 + jnp.tanh(math.sqrt(2.0 / math.pi) * (x + 0.044715 * (x * x * x))))


S5_GB = LANES // S5_GROUP


def _s5_kernel(u_ref, bre_ref, bim_ref, cre_ref, cim_ref, lr_r_ref, li_r_ref, dt_r_ref,
               lr_c_ref, li_c_ref, dt_c_ref, d_ref, o_ref, ua_s, bps_s, cq_s, ks_s, y_s, *, nc):
    step, grp, st = S5_STEP, S5_GROUP, S5_STATE
    gw = S5_GB * grp
    half = S5_GB * st
    sw = 2 * half
    r = ua_s.shape[0]
    gsh, ssh = grp.bit_length() - 1, st.bit_length() - 1

    lr, li = lr_r_ref[0], li_r_ref[0]
    dt = jnp.exp(dt_r_ref[0])
    ar, ai = _cexp(lr * dt, li * dt)
    den = lr * lr + li * li
    pr = ((ar - 1.0) * lr + ai * li) / den
    pi_ = (ai * lr - (ar - 1.0) * li) / den
    re_lane = lax.broadcasted_iota(jnp.int32, (1, sw), 1) < half
    row_g = lax.shift_right_logical(lax.broadcasted_iota(jnp.int32, (gw, sw), 0), gsh)
    lane_g = lax.shift_right_logical(lax.broadcasted_iota(jnp.int32, (gw, sw), 1) & (half - 1), ssh)
    same_g = row_g == lane_g
    br = jnp.where(same_g, jnp.concatenate([bre_ref[0]] * (sw // LANES), axis=1), 0.0)
    bi = jnp.where(same_g, jnp.concatenate([bim_ref[0]] * (sw // LANES), axis=1), 0.0)
    bp0 = None
    for k in range(step):
        blk = br * jnp.where(re_lane, pr, pi_) + bi * jnp.where(re_lane, -pi_, pr)
        if k == 0:
            bp0 = blk
        s = step - 1 - k
        bps_s[s * gw:(s + 1) * gw, :] = blk.astype(BF16)
        pr, pi_ = _cmul(pr, pi_, ar, ai)

    dtc = jnp.exp(dt_c_ref[0])
    acr_d, aci_d = _cexp(lr_c_ref[0] * dtc, li_c_ref[0] * dtc)
    hr = half // LANES

    def spread(v):
        vt = jnp.concatenate([v, jnp.zeros((LANES - hr, LANES), F32)], axis=0).T
        return jnp.concatenate([jnp.broadcast_to(vt[:, q:q + 1], (LANES, gw)) for q in range(hr)], axis=0)

    acr, aci = spread(acr_d), spread(aci_d)
    same_c = (lax.shift_right_logical(lax.broadcasted_iota(jnp.int32, (half, gw), 0), ssh)
              == lax.shift_right_logical(lax.broadcasted_iota(jnp.int32, (half, gw), 1), gsh))
    pcr = jnp.where(same_c, cre_ref[0], 0.0)
    pci = jnp.where(same_c, cim_ref[0], 0.0)
    bp_hi = bp0.astype(BF16)
    bp_lo = (bp0 - bp_hi.astype(F32)).astype(BF16)
    for k in range(step + 1):
        re_hi, im_hi = pcr.astype(BF16), (-pci).astype(BF16)
        cq_hi = jnp.concatenate([re_hi, im_hi], axis=0)
        if k < step:
            cq_lo = jnp.concatenate([(pcr - re_hi.astype(F32)).astype(BF16),
                                     (-pci - im_hi.astype(F32)).astype(BF16)], axis=0)
            taps = _dot(bp_hi, cq_hi) + (_dot(bp_hi, cq_lo) + _dot(bp_lo, cq_hi))
            ks_s[(step - 1 - k) * gw:(step - k) * gw, :] = taps.astype(BF16)
            pcr, pci = _cmul(pcr, pci, acr, aci)
        if k >= 1:
            cq_s[k - 1] = cq_hi

    for s in range(step):
        ua_s[:, s * gw:(s + 1) * gw] = u_ref[pl.ds(s, r, stride=step), :].astype(BF16)
    x = _dot(ua_s[...], bps_s[...])
    blk_i = lax.broadcasted_iota(jnp.int32, (r, sw), 0) & (nc - 1)
    er, ei = ar, ai
    for _ in range(step.bit_length() - 1):
        er, ei = _cmul(er, ei, er, ei)
    sh = 1
    while sh < nc:
        xs = jnp.where(blk_i >= sh, pltpu.roll(x, sh, axis=0), 0.0)
        x = x + xs * er + pltpu.roll(xs, half, axis=1) * jnp.where(re_lane, -ei, ei)
        er, ei = _cmul(er, ei, er, ei)
        sh *= 2
    xp = jnp.where(blk_i >= 1, pltpu.roll(x, 1, axis=0), 0.0).astype(BF16)

    d = d_ref[0]
    for t in range(step):
        y = (_dot(ua_s[:, :(t + 1) * gw], ks_s[(step - 1 - t) * gw:, :]) + _dot(xp, cq_s[t])
             + d * u_ref[pl.ds(t, r, stride=step), :])
        y_s[pl.ds(t, r, stride=step), :] = _gelu_tanh(y)

    rt = 512

    def emit(i, carry):
        r0 = pl.multiple_of(i * rt, rt)
        o_ref[pl.ds(r0, rt), :] = y_s[pl.ds(r0, rt), :].astype(o_ref.dtype)
        return carry

    lax.fori_loop(0, y_s.shape[0] // rt, emit, 0)


def _s5(u2, a_re, a_im, b_re, b_im, c_re, c_im, d, log_dt, seq):
    groups, st = a_re.shape
    grp, step = S5_GROUP, S5_STEP
    m = u2.shape[0]
    g8 = groups // S5_GB
    gw, half = S5_GB * grp, S5_GB * st
    sw = 2 * half
    nc = seq // step
    assert nc & (nc - 1) == 0 and gw == LANES
    r = m // step
    bt = lambda t: jnp.tile(t.transpose(0, 2, 1).reshape(g8, gw, st), (1, 1, 2))
    ct = lambda t: jnp.tile(t.transpose(0, 2, 1).reshape(g8, half, grp), (1, 1, S5_GB))
    rowv = lambda t: jnp.tile(t.reshape(g8, 1, half), (1, 1, 2))
    colv = lambda t: t.reshape(g8, half // LANES, LANES)
    ldt = jnp.broadcast_to(log_dt[:, None], (groups, st))
    g3 = lambda shp: pl.BlockSpec((1,) + shp, lambda g: (g, 0, 0))
    dense = g3((half // LANES, LANES))
    return pl.pallas_call(
        functools.partial(_s5_kernel, nc=nc),
        grid=(g8,),
        in_specs=[pl.BlockSpec((m, gw), lambda g: (0, g)),
                  g3((gw, 2 * st)), g3((gw, 2 * st)), g3((half, gw)), g3((half, gw)),
                  g3((1, sw)), g3((1, sw)), g3((1, sw)), dense, dense, dense,
                  g3((1, gw))],
        out_specs=pl.BlockSpec((m, gw), lambda g: (0, g)),
        out_shape=jax.ShapeDtypeStruct((m, groups * grp), BF16),
        scratch_shapes=[pltpu.VMEM((r, step * gw), BF16), pltpu.VMEM((step * gw, sw), BF16),
                        pltpu.VMEM((step, sw, gw), BF16), pltpu.VMEM((step * gw, gw), BF16),
                        pltpu.VMEM((m, gw), F32)],
        compiler_params=_params("arbitrary"),
        name="s5_ssm",
    )(u2, bt(b_re), bt(b_im), ct(c_re), ct(c_im), rowv(a_re), rowv(a_im), rowv(ldt),
      colv(a_re), colv(a_im), colv(ldt), d.reshape(g8, 1, gw))


def _sconv_kernel(gb_ref, gc_ref, xc_ref, w_ref, o_ref):
    p = gc_ref[...].astype(F32) * xc_ref[...].astype(F32)
    w = w_ref[...]
    row = lax.broadcasted_iota(jnp.int32, p.shape, 0)
    acc = p * w[C_CONV - 1:C_CONV, :]
    for s in range(1, C_CONV):
        acc = acc + _shift_rows(p, s, row) * w[C_CONV - 1 - s:C_CONV - s, :]
    o_ref[...] = (gb_ref[...].astype(F32) * acc).astype(o_ref.dtype)


def _sconv(proj, conv_w, bsz, seq):
    width = conv_w.shape[1]
    tc = 256
    nb = width // tc
    col = lambda off: pl.BlockSpec((seq, tc), lambda b, j, off=off: (b, off * nb + j))
    return pl.pallas_call(
        _sconv_kernel,
        grid=(bsz, nb),
        in_specs=[col(0), col(1), col(2), pl.BlockSpec((C_CONV, tc), lambda b, j: (0, j))],
        out_specs=pl.BlockSpec((seq, tc), lambda b, j: (b, j)),
        out_shape=jax.ShapeDtypeStruct((bsz * seq, width), BF16),
        compiler_params=_params("arbitrary", "arbitrary"),
        name="short_conv",
    )(proj, proj, proj, conv_w)


NEG_BIG = -1e30
LOG2_E = 1.4426950408889634


def _dattn_kernel(q_ref, k_ref, vt_ref, lq1_ref, lk1_ref, lq2_ref, lk2_ref, ng_ref, o_ref,
                  acc_s, *, heads, tq, lambda_init):
    dh = DA_HEAD_DIM
    qi = pl.program_id(1)
    scale = dh ** -0.5
    lam = (jnp.exp(jnp.sum(lq1_ref[...] * lk1_ref[...], axis=-1, keepdims=True))
           - jnp.exp(jnp.sum(lq2_ref[...] * lk2_ref[...], axis=-1, keepdims=True)) + lambda_init)
    ki = lax.broadcasted_iota(jnp.int32, (tq, tq), 0)
    qj = lax.broadcasted_iota(jnp.int32, (tq, tq), 1)
    kpos = ki.astype(F32)
    visible = ki <= qj
    ng = ng_ref[...]

    hpi = 4
    for h0 in range(0, heads, hpi):
        slopes = [2.0 ** (-8.0 * (h + 1) / heads) for h in range(h0, h0 + hpi)]

        key_bias = [(slopes[t] * LOG2_E) * kpos for t in range(hpi)]

        def block(j, carry, masked, h0=h0, slopes=slopes, key_bias=key_bias):
            k0 = pl.multiple_of(j * tq, tq)
            k0f = k0.astype(F32)
            out = []
            for t in range(hpi):
                v_lo = (h0 + t) * 2 * dh
                vt = vt_ref[0, v_lo:v_lo + 2 * dh, pl.ds(k0, tq)]
                off = (slopes[t] * LOG2_E) * k0f
                for mp in range(2):
                    c = 2 * t + mp
                    m, l = carry[2 * c], carry[2 * c + 1]
                    lo = v_lo + mp * dh
                    s = (_dot_nt(k_ref[pl.ds(k0, tq), lo:lo + dh], q_ref[:, lo:lo + dh]) * (scale * LOG2_E)
                         + key_bias[t])
                    if masked:
                        s = jnp.where(visible, s, NEG_BIG)
                    m_new = jnp.maximum(m, jnp.max(s, axis=0, keepdims=True) + off)
                    corr = jnp.exp2(m - m_new)
                    p = jnp.exp2(s + (off - m_new))
                    out += [m_new, corr * l + jnp.sum(p, axis=0, keepdims=True)]
                    acc_s[c] = corr * acc_s[c] + _dot(vt, p.astype(BF16))
            return tuple(out)

        acc_s[...] = jnp.zeros(acc_s.shape, F32)
        neg = jnp.full((1, tq), NEG_BIG, F32)
        zero = jnp.zeros((1, tq), F32)
        carry = lax.fori_loop(0, qi, functools.partial(block, masked=False), (neg, zero) * (2 * hpi))
        carry = block(qi, carry, True)
        for t in range(hpi):
            v_lo = (h0 + t) * 2 * dh
            l1, l2 = carry[4 * t + 1], carry[4 * t + 3]
            o = acc_s[2 * t] / l1 - lam * (acc_s[2 * t + 1] / l2)
            o = o * lax.rsqrt(jnp.mean(o * o, axis=0, keepdims=True) + EPS)
            o_ref[:, v_lo:v_lo + 2 * dh] = (o.T * ng * (1.0 - lambda_init)).astype(o_ref.dtype)


def _dattn(proj, vt, lq1, lk1, lq2, lk2, norm_g, lambda_init, bsz, seq):
    dh = DA_HEAD_DIM
    width = vt.shape[1]
    heads = width // (2 * dh)
    tq = min(256, seq)
    nq = seq // tq
    vec = pl.BlockSpec((1, dh), lambda b, i: (0, 0))
    return pl.pallas_call(
        functools.partial(_dattn_kernel, heads=heads, tq=tq, lambda_init=lambda_init),
        grid=(bsz, nq),
        in_specs=[pl.BlockSpec((tq, width), lambda b, i: (b * nq + i, 3)),
                  pl.BlockSpec((seq, width), lambda b, i: (b, 4)),
                  pl.BlockSpec((1, width, seq), lambda b, i: (b, 0, 0)),
                  vec, vec, vec, vec, pl.BlockSpec((1, 2 * dh), lambda b, i: (0, 0))],
        out_specs=pl.BlockSpec((tq, width), lambda b, i: (b * nq + i, 0)),
        out_shape=jax.ShapeDtypeStruct((bsz * seq, width), BF16),
        scratch_shapes=[pltpu.VMEM((8, 2 * dh, tq), F32)],
        compiler_params=_params("arbitrary", "arbitrary"),
        name="diff_attention",
    )(proj, proj, vt, lq1.reshape(1, dh), lk1.reshape(1, dh), lq2.reshape(1, dh), lk2.reshape(1, dh),
      norm_g.reshape(1, 2 * dh))


def kernel(x, c, ada_w, ada_b, ln1_g, ln1_b, ln2_g, ln2_b, ffn_w_gate, ffn_w_up, ffn_w_down, ab_w_in, ab_w_out, dn_conv_w, dn_a_log, dn_dt_bias, dn_norm_g, s5_a_re, s5_a_im, s5_b_re, s5_b_im, s5_c_re, s5_c_im, s5_d, s5_log_dt, s5_w_glu, cd_w_in, cd_w_out, sc_conv_w, da_lq1, da_lk1, da_lq2, da_lk2, da_norm_g):
    bsz, seq, d = x.shape
    depth = ada_w.shape[0]
    m = bsz * seq
    a_width = dn_conv_w.shape[-1] // 3
    heads = dn_a_log.shape[-1]

    c_pad = jnp.zeros((8, d), F32).at[:bsz].set(c)
    mod = _ada_mod(c_pad, ada_w, ada_b)

    ab_out, cd_out, w_down = ab_w_out.astype(BF16), cd_w_out.astype(BF16), ffn_w_down.astype(BF16)
    x2 = x.reshape(m, d)
    h = _modulate(x2, mod, (0, 1), (0, 0), seq)
    for i in range(depth):
        j = i // 2
        if i % 2 == 0:
            w_uba = jnp.pad(jnp.concatenate([ab_w_in[j:j + 1, :, 4 * a_width + 2 * heads:],
                                             ab_w_in[j:j + 1, :, 4 * a_width:4 * a_width + 2 * heads]], axis=2),
                            ((0, 0), (0, 0), (0, LANES - 2 * heads)))
            b_width = w_uba.shape[2] - LANES
            proj = _mm(h, ab_w_in, j, 4 * a_width, BF16, 1024, 1024, "ab_in_proj")
            uf = _mm(h, w_uba, 0, w_uba.shape[2], F32, 1024, w_uba.shape[2], "ab_u_proj")
            ya = _deltanet(proj, uf[:, b_width:], dn_conv_w[j], dn_a_log[j], dn_dt_bias[j], dn_norm_g[j], bsz, seq)
            yb = _s5(uf, s5_a_re[j], s5_a_im[j], s5_b_re[j], s5_b_im[j],
                     s5_c_re[j], s5_c_im[j], s5_d[j], s5_log_dt[j], seq)
            yb = _glu(yb, s5_w_glu[j].astype(BF16), 1024)
            w_out = ab_out
        else:
            lambda_init = 0.8 - 0.6 * math.exp(-0.3 * i)
            c_width = sc_conv_w.shape[-1]
            proj = _mm(h, cd_w_in, j, 5 * c_width, BF16, 1024, 1024, "cd_in_proj")
            vt = _mm_t(h, cd_w_in[j, :, 5 * c_width:].T, bsz, seq, 512, 1024, "cd_value_proj")
            ya = _sconv(proj, sc_conv_w[j], bsz, seq)
            yb = _dattn(proj, vt, da_lq1[j], da_lk1[j], da_lq2[j], da_lk2[j], da_norm_g[j], lambda_init, bsz, seq)
            w_out = cd_out
        x2, h = _res_ln([ya, yb], w_out, j, x2, mod, (i, 2), ln1_g[i], ln1_b[i],
                        ((i, 4), (i, 3)), seq, 512, "mixer_out_ln")
        act = _gate_up(h, ffn_w_gate, ffn_w_up, i, 1024, 512)
        nxt = ((i + 1, 1), (i + 1, 0)) if i + 1 < depth else None
        x2, h = _res_ln([act], w_down, i, x2, mod, (i, 5), ln2_g[i], ln2_b[i],
                        nxt, seq, 256, "ffn_down_ln")
    return x2.reshape(bsz, seq, d)
```
